```python
import jax, jax.numpy as jnp
from jax import lax
import numpy as np

D_MODEL = 1024
BATCH = 2
SEQ = 8192
DEPTH = 1

N_MEM = 256
HEAD_DIM = 128
HEADS_PER_GROUP = 4
DIL_GROUPS = ((128, 1), (512, 4), (2048, 16))
N_GROUPS = len(DIL_GROUPS)
ATTN_HEADS = N_GROUPS * HEADS_PER_GROUP
ATTN_WIDTH = ATTN_HEADS * HEAD_DIM
ATTN_OUT = HEADS_PER_GROUP * HEAD_DIM
ROT_DIM = HEAD_DIM // 4
ROPE_THETA = 500000.0
CONV_CH = 3 * D_MODEL // 4
CONV_K = 31
N_BRANCH = 2
IN_SPLITS = (ATTN_WIDTH, ATTN_WIDTH, ATTN_WIDTH, CONV_CH, CONV_CH, D_MODEL, D_MODEL)
IN_WIDTH = sum(IN_SPLITS)
CROSS_HEADS = 4
CROSS_HEAD_DIM = D_MODEL // CROSS_HEADS
D_FF = 4 * D_MODEL
EPS = 1e-6

kernel_name = 'hybrid_dilated_attn_conformer_conv_gated'


def rmsnorm(x, g):
    xf = x.astype(jnp.float32)
    y = xf * lax.rsqrt(jnp.mean(xf * xf, axis=-1, keepdims=True) + EPS) * g.astype(jnp.float32)
    return y.astype(x.dtype)


def layernorm(x, g, b):
    xf = x.astype(jnp.float32)
    mu = jnp.mean(xf, axis=-1, keepdims=True)
    var = jnp.mean(jnp.square(xf - mu), axis=-1, keepdims=True)
    y = (xf - mu) * lax.rsqrt(var + EPS) * g.astype(jnp.float32) + b.astype(jnp.float32)
    return y.astype(x.dtype)


def rope_partial(t, pos):
    half = ROT_DIM // 2
    inv_freq = ROPE_THETA ** (-jnp.arange(0, ROT_DIM, 2, dtype=jnp.float32) / ROT_DIM)
    ang = pos[:, None] * inv_freq[None, :]
    cos = jnp.cos(ang)[None, :, None, :]
    sin = jnp.sin(ang)[None, :, None, :]
    tf = t.astype(jnp.float32)
    x1, x2, rest = tf[..., :half], tf[..., half:ROT_DIM], tf[..., ROT_DIM:]
    out = jnp.concatenate([x1 * cos - x2 * sin, x2 * cos + x1 * sin, rest], axis=-1)
    return out.astype(t.dtype)


def dilated_window_attention(q, k, v, window, dilation):
    B, S, H, Dh = q.shape
    L = window // dilation
    span = dilation * L
    Sp = -(-S // span) * span
    M = Sp // dilation
    nb = M // L
    pad = ((0, 0), (0, Sp - S), (0, 0), (0, 0))

    def to_blocks(t):
        t = jnp.pad(t, pad).reshape(B, M, dilation, H, Dh)
        t = t.transpose(0, 3, 2, 1, 4)
        return t.reshape(B, H, dilation, nb, L, Dh)

    def with_prev(t):
        prev = jnp.pad(t, ((0, 0), (0, 0), (0, 0), (1, 0), (0, 0), (0, 0)))[:, :, :, :-1]
        return jnp.concatenate([prev, t], axis=-2)

    qb = to_blocks(q)
    kw = with_prev(to_blocks(k))
    vw = with_prev(to_blocks(v))
    s = jnp.einsum('bhrnqe,bhrnke->bhrnqk', qb, kw).astype(jnp.float32) * (Dh ** -0.5)
    qi = jnp.arange(L)[:, None]
    kj = jnp.arange(2 * L)[None, :]
    band = (kj >= qi) & (kj <= qi + L)
    first = (jnp.arange(nb)[:, None, None] == 0) & (kj[None] < L)
    mask = band[None] & jnp.logical_not(first)
    s = jnp.where(mask, s, -jnp.inf)
    lse = jax.nn.logsumexp(s, axis=-1)
    p = jnp.exp(s - lse[..., None])
    o = jnp.einsum('bhrnqk,bhrnke->bhrnqe', p.astype(v.dtype), vw)
    o = o.reshape(B, H, dilation, M, Dh).transpose(0, 3, 2, 1, 4).reshape(B, Sp, H, Dh)[:, :S]
    lse = lse.reshape(B, H, dilation, M).transpose(0, 3, 2, 1).reshape(B, Sp, H)[:, :S]
    return o, lse


def setup_inputs(seed: int = 0) -> dict:
    key = jax.random.key(seed)
    ks = jax.random.split(key, 24)
    f32 = jnp.float32
    nrm = lambda k, shape, fan: jax.random.normal(k, shape, f32) * (fan ** -0.5)
    gain = lambda k, shape: 1.0 + 0.01 * jax.random.normal(k, shape, f32)
    small = lambda k, shape: 0.01 * jax.random.normal(k, shape, f32)
    return {
        'x': jax.random.normal(ks[0], (BATCH, SEQ, D_MODEL), f32),
        'mem': jax.random.normal(ks[1], (BATCH, N_MEM, D_MODEL), f32),
        'g_mix': gain(ks[2], (DEPTH, D_MODEL)),
        'w_in': nrm(ks[3], (DEPTH, D_MODEL, IN_WIDTH), D_MODEL),
        'b_gate': small(ks[4], (DEPTH, N_BRANCH * D_MODEL)),
        'conv_w': nrm(ks[5], (DEPTH, CONV_K, CONV_CH), CONV_K),
        'conv_b': small(ks[6], (DEPTH, CONV_CH)),
        'conv_ln_g': gain(ks[7], (DEPTH, CONV_CH)),
        'conv_ln_b': small(ks[8], (DEPTH, CONV_CH)),
        'w_attn_proj': nrm(ks[9], (DEPTH, ATTN_OUT, D_MODEL), ATTN_OUT),
        'w_conv_proj': nrm(ks[10], (DEPTH, CONV_CH, D_MODEL), CONV_CH),
        'w_out': nrm(ks[11], (DEPTH, D_MODEL, D_MODEL), D_MODEL),
        'g_cross': gain(ks[12], (DEPTH, D_MODEL)),
        'g_mem': gain(ks[13], (DEPTH, D_MODEL)),
        'w_cq': nrm(ks[14], (DEPTH, D_MODEL, D_MODEL), D_MODEL),
        'w_ckv': nrm(ks[15], (DEPTH, D_MODEL, 2 * D_MODEL), D_MODEL),
        'w_co': nrm(ks[16], (DEPTH, D_MODEL, D_MODEL), D_MODEL),
        'g_mlp': gain(ks[17], (DEPTH, D_MODEL)),
        'w_up': nrm(ks[18], (DEPTH, D_MODEL, D_FF), D_MODEL),
        'w_down': nrm(ks[19], (DEPTH, D_FF, D_MODEL), D_FF),
        'g_final': gain(ks[20], (D_MODEL,)),
    }


def reference(x, mem, g_mix, w_in, b_gate, conv_w, conv_b, conv_ln_g, conv_ln_b, w_attn_proj,
              w_conv_proj, w_out, g_cross, g_mem, w_cq, w_ckv, w_co, g_mlp, w_up, w_down, g_final):
    B, S, _ = x.shape
    pos = jnp.arange(S, dtype=jnp.float32)
    split_pts = [int(v) for v in np.cumsum(IN_SPLITS)[:-1]]
    for l in range(DEPTH):
        u = rmsnorm(x, g_mix[l])
        z = u @ w_in[l]
        q, k, v, glu_a, glu_b, gate_a, gate_b = jnp.split(z, split_pts, axis=-1)
        q = rope_partial(q.reshape(B, S, ATTN_HEADS, HEAD_DIM), pos)
        k = rope_partial(k.reshape(B, S, ATTN_HEADS, HEAD_DIM), pos)
        v = v.reshape(B, S, ATTN_HEADS, HEAD_DIM)
        q = q.reshape(B, S, N_GROUPS, HEADS_PER_GROUP, HEAD_DIM)
        k = k.reshape(B, S, N_GROUPS, HEADS_PER_GROUP, HEAD_DIM)
        v = v.reshape(B, S, N_GROUPS, HEADS_PER_GROUP, HEAD_DIM)
        outs, lses = [], []
        for g, (win, dil) in enumerate(DIL_GROUPS):
            o_g, lse_g = dilated_window_attention(q[:, :, g], k[:, :, g], v[:, :, g], win, dil)
            outs.append(o_g)
            lses.append(lse_g)
        wts = jax.nn.softmax(jnp.stack(lses, axis=0), axis=0)
        attn = jnp.sum(wts[..., None] * jnp.stack(outs, axis=0).astype(jnp.float32), axis=0)
        y_attn = attn.astype(x.dtype).reshape(B, S, ATTN_OUT) @ w_attn_proj[l]

        c = glu_a * jax.nn.sigmoid(glu_b)
        c = lax.conv_general_dilated(c, conv_w[l].astype(c.dtype)[:, None, :], window_strides=(1,),
                                     padding=[(CONV_K - 1, 0)], dimension_numbers=('NWC', 'WIO', 'NWC'),
                                     feature_group_count=CONV_CH) + conv_b[l]
        c = jax.nn.silu(layernorm(c, conv_ln_g[l], conv_ln_b[l]))
        y_conv = c @ w_conv_proj[l]

        bg_a, bg_b = jnp.split(b_gate[l], 2)
        merged = jax.nn.sigmoid(gate_a + bg_a) * y_attn + jax.nn.sigmoid(gate_b + bg_b) * y_conv
        x = x + merged @ w_out[l]

        uq = rmsnorm(x, g_cross[l])
        m = rmsnorm(mem, g_mem[l])
        cq = (uq @ w_cq[l]).reshape(B, S, CROSS_HEADS, CROSS_HEAD_DIM)
        ck, cv = jnp.split(m @ w_ckv[l], 2, axis=-1)
        ck = ck.reshape(B, N_MEM, CROSS_HEADS, CROSS_HEAD_DIM)
        cv = cv.reshape(B, N_MEM, CROSS_HEADS, CROSS_HEAD_DIM)
        sc = jnp.einsum('bshe,bmhe->bhsm', cq, ck).astype(jnp.float32) * (CROSS_HEAD_DIM ** -0.5)
        pc = jax.nn.softmax(sc, axis=-1).astype(cv.dtype)
        co = jnp.einsum('bhsm,bmhe->bshe', pc, cv).reshape(B, S, D_MODEL)
        x = x + co @ w_co[l]

        h = jnp.square(jax.nn.relu(rmsnorm(x, g_mlp[l]) @ w_up[l]))
        x = x + h @ w_down[l]
    return rmsnorm(x, g_final)
```

```python
import functools

import jax
import jax.numpy as jnp
from jax import lax
from jax.experimental import pallas as pl
from jax.experimental.pallas import tpu as pltpu

F32 = jnp.float32
BF16 = jnp.bfloat16

HEAD_DIM = 128
HEADS_PER_GROUP = 4
DIL_GROUPS = ((128, 1), (512, 4), (2048, 16))
N_GROUPS = len(DIL_GROUPS)
GROUP_WIDTH = HEADS_PER_GROUP * HEAD_DIM
ROT_DIM = HEAD_DIM // 4
ROPE_THETA = 500000.0
CONV_K = 31
CROSS_HEADS = 4
EPS = 1e-6

LANES = 128
BAND = 128
CONV_HALO = 32
MASK_VALUE = -1e30
VMEM_LIMIT = 56 * 1024 * 1024

for _win, _dil in DIL_GROUPS:
    assert _win // _dil == BAND


def _const_spec(shape):
    return pl.BlockSpec(shape, lambda *_: (0,) * len(shape), pipeline_mode=pl.Buffered(1))


def _rms_scale(xf):
    return lax.rsqrt(jnp.mean(xf * xf, axis=-1, keepdims=True) + EPS)


def _sigmoid(z):
    return 1.0 / (1.0 + jnp.exp(-z))


def _dot(a, b):
    return jnp.dot(a, b, preferred_element_type=F32)


def _dot_nt(a, b):
    return lax.dot_general(a, b, (((1,), (1,)), ((), ())), preferred_element_type=F32)


def _norm_to_scratch(x_ref, g_ref, u_ref):
    @pl.when(pl.program_id(1) == 0)
    def _():
        xf = x_ref[...]
        u_ref[...] = (xf * _rms_scale(xf) * g_ref[...]).astype(BF16)


def _qkv_kernel(x_ref, g_ref, w_ref, tab_ref, o_ref, u_ref):
    _norm_to_scratch(x_ref, g_ref, u_ref)
    j = pl.program_id(1)
    z = _dot(u_ref[...], w_ref[...])

    @pl.when(j < 2 * N_GROUPS)
    def _():
        cos = tab_ref[0]
        sin_hi = tab_ref[1]
        sin_lo = tab_ref[2]
        for h in range(HEADS_PER_GROUP):
            t = z[:, h * HEAD_DIM:(h + 1) * HEAD_DIM]
            up = pltpu.roll(t, ROT_DIM // 2, 1)
            down = pltpu.roll(t, HEAD_DIM - ROT_DIM // 2, 1)
            o_ref[h] = (t * cos + up * sin_hi + down * sin_lo).astype(BF16)

    @pl.when(j >= 2 * N_GROUPS)
    def _():
        for h in range(HEADS_PER_GROUP):
            o_ref[h] = z[:, h * HEAD_DIM:(h + 1) * HEAD_DIM].astype(BF16)


def _glu_kernel(x_ref, g_ref, w_ref, o_ref, u_ref):
    _norm_to_scratch(x_ref, g_ref, u_ref)
    z = _dot(u_ref[...], w_ref[...])
    half = z.shape[1] // 2
    o_ref[...] = (z[:, :half] * _sigmoid(z[:, half:])).astype(BF16)


def _gate_kernel(x_ref, g_ref, w_ref, b_ref, o_ref, u_ref):
    _norm_to_scratch(x_ref, g_ref, u_ref)
    z = _dot(u_ref[...], w_ref[...])
    o_ref[...] = _sigmoid(z + b_ref[...]).astype(BF16)


def _rope_tables(seq, scale):
    half = ROT_DIM // 2
    pos = jnp.arange(seq, dtype=F32)
    inv_freq = ROPE_THETA ** (-jnp.arange(0, ROT_DIM, 2, dtype=F32) / ROT_DIM)
    ang = pos[:, None] * inv_freq[None, :]
    cos, sin = jnp.cos(ang), jnp.sin(ang)
    zeros = jnp.zeros((seq, HEAD_DIM - ROT_DIM), F32)
    zhalf = jnp.zeros((seq, half), F32)
    c = jnp.concatenate([cos, cos, zeros + 1.0], axis=1)
    s_hi = jnp.concatenate([zhalf, sin, zeros], axis=1)
    s_lo = jnp.concatenate([-sin, zhalf, zeros], axis=1)
    return jnp.stack([c, s_hi, s_lo]) * scale


def _in_projection(x, g_mix, w_in, b_gate, tm=1024):
    B, S, D = x.shape
    T = B * S
    x2 = x.reshape(T, D)
    g2 = g_mix.reshape(1, D)
    attn_w = N_GROUPS * GROUP_WIDTH
    conv_ch = (w_in.shape[1] - 3 * attn_w - 2 * D) // 2
    wb = w_in.astype(BF16)
    w_qkv = wb[:, :3 * attn_w]
    glu_tile = 256
    n_glu = conv_ch // glu_tile
    w_a = wb[:, 3 * attn_w:3 * attn_w + conv_ch].reshape(D, n_glu, glu_tile)
    w_b = wb[:, 3 * attn_w + conv_ch:3 * attn_w + 2 * conv_ch].reshape(D, n_glu, glu_tile)
    w_glu = jnp.concatenate([w_a, w_b], axis=2).reshape(D, 2 * conv_ch)
    w_gate = wb[:, 3 * attn_w + 2 * conv_ch:]
    tiles_per_seq = S // tm
    params = pltpu.CompilerParams(dimension_semantics=("parallel", "arbitrary"),
                                  vmem_limit_bytes=VMEM_LIMIT)
    x_spec = pl.BlockSpec((tm, D), lambda i, j: (i, 0))
    g_spec = pl.BlockSpec((1, D), lambda i, j: (0, 0))
    u_scratch = pltpu.VMEM((tm, D), BF16)

    tabs = jnp.stack([_rope_tables(S, HEAD_DIM ** -0.5), _rope_tables(S, 1.0)])
    qkv = pl.pallas_call(
        _qkv_kernel,
        grid=(T // tm, 3 * N_GROUPS),
        in_specs=[x_spec, g_spec,
                  pl.BlockSpec((D, GROUP_WIDTH), lambda i, j: (0, j)),
                  pl.BlockSpec((None, 3, tm, HEAD_DIM),
                               lambda i, j: (jnp.minimum(j // N_GROUPS, 1), 0, i % tiles_per_seq, 0))],
        out_specs=pl.BlockSpec(
            (None, None, None, HEADS_PER_GROUP, tm, HEAD_DIM),
            lambda i, j: (j // N_GROUPS, j % N_GROUPS, i // tiles_per_seq, 0, i % tiles_per_seq, 0)),
        out_shape=jax.ShapeDtypeStruct((3, N_GROUPS, B, HEADS_PER_GROUP, S, HEAD_DIM), BF16),
        scratch_shapes=[u_scratch],
        compiler_params=params,
        name="in_proj_qkv",
    )(x2, g2, w_qkv, tabs)

    c = pl.pallas_call(
        _glu_kernel,
        grid=(T // tm, n_glu),
        in_specs=[x_spec, g_spec, pl.BlockSpec((D, 2 * glu_tile), lambda i, j: (0, j))],
        out_specs=pl.BlockSpec((tm, glu_tile), lambda i, j: (i, j)),
        out_shape=jax.ShapeDtypeStruct((T, conv_ch), BF16),
        scratch_shapes=[u_scratch],
        compiler_params=params,
        name="in_proj_glu",
    )(x2, g2, w_glu)

    gate_tile = 512
    gates = pl.pallas_call(
        _gate_kernel,
        grid=(T // tm, 2 * D // gate_tile),
        in_specs=[x_spec, g_spec,
                  pl.BlockSpec((D, gate_tile), lambda i, j: (0, j)),
                  pl.BlockSpec((1, gate_tile), lambda i, j: (0, j))],
        out_specs=pl.BlockSpec((tm, gate_tile), lambda i, j: (i, j)),
        out_shape=jax.ShapeDtypeStruct((T, 2 * D), BF16),
        scratch_shapes=[u_scratch],
        compiler_params=params,
        name="in_proj_gate",
    )(x2, g2, w_gate, b_gate.reshape(1, 2 * D))
    return qkv, c.reshape(B, S, conv_ch), gates


def _attn_kernel(q_ref, k_ref, kp_ref, v_ref, vp_ref, o_ref, st_ref, *, dil):
    rows = q_ref.shape[1]
    n_chunks = rows // BAND
    tile = pl.program_id(1)
    r_id = lax.broadcasted_iota(jnp.int32, (BAND, 2 * BAND), 0)
    c_id = lax.broadcasted_iota(jnp.int32, (BAND, 2 * BAND), 1)
    band = jnp.logical_and(c_id >= r_id, c_id <= r_id + BAND)
    band_first = jnp.logical_and(band, jnp.logical_or(c_id >= BAND, tile > 0))
    lane = lax.broadcasted_iota(jnp.int32, (BAND, LANES), 1)
    st_ref[...] = jnp.zeros(st_ref.shape, F32)

    def head_body(h, carry):
        for n in range(n_chunks):
            rs = slice(n * BAND, (n + 1) * BAND)
            for r in range(dil):
                cs = slice(r * HEAD_DIM, (r + 1) * HEAD_DIM)
                q = q_ref[h, rs, cs]
                if n == 0:
                    k = jnp.concatenate([kp_ref[h, :, cs], k_ref[h, rs, cs]], axis=0)
                    v = jnp.concatenate([vp_ref[h, :, cs], v_ref[h, rs, cs]], axis=0)
                    mask = band_first
                else:
                    ks = slice((n - 1) * BAND, (n + 1) * BAND)
                    k = k_ref[h, ks, cs]
                    v = v_ref[h, ks, cs]
                    mask = band
                s = jnp.where(mask, _dot_nt(q, k), MASK_VALUE)
                m = jnp.max(s, axis=-1, keepdims=True)
                p = jnp.exp(s - m)
                l = jnp.sum(p, axis=-1, keepdims=True)
                acc = _dot(p.astype(BF16), v)
                o_ref[h, rs, cs] = (acc * (1.0 / l)).astype(BF16)
                lse = m + jnp.log(l)
                st_ref[rs, cs] = jnp.where(lane == h, lse, st_ref[rs, cs])
        return carry

    lax.fori_loop(0, HEADS_PER_GROUP, head_body, 0)


def _dilated_attention(qkv, group, dil, tile_tokens=2048):
    _, _, B, H, S, Dh = qkv.shape
    rows = tile_tokens // dil
    width = dil * Dh
    view = qkv.reshape(3, N_GROUPS, B, H, S // dil, width)
    chunks = rows // BAND

    def cur(t):
        return pl.BlockSpec((None, None, None, H, rows, width),
                            lambda b, i: (t, group, b, 0, i, 0))

    def prev(t):
        return pl.BlockSpec((None, None, None, H, BAND, width),
                            lambda b, i: (t, group, b, 0, jnp.maximum(i * chunks - 1, 0), 0))

    o, st = pl.pallas_call(
        functools.partial(_attn_kernel, dil=dil),
        grid=(B, S // tile_tokens),
        in_specs=[cur(0), cur(1), prev(1), cur(2), prev(2)],
        out_specs=[pl.BlockSpec((None, H, rows, width), lambda b, i: (b, 0, i, 0)),
                   pl.BlockSpec((None, rows, width), lambda b, i: (b, i, 0))],
        out_shape=[jax.ShapeDtypeStruct((B, H, S // dil, width), BF16),
                   jax.ShapeDtypeStruct((B, S // dil, width), F32)],
        compiler_params=pltpu.CompilerParams(dimension_semantics=("parallel", "arbitrary"),
                                             vmem_limit_bytes=VMEM_LIMIT),
        name=f"dilated_attn_d{dil}",
    )(view, view, view, view, view)
    return o.reshape(B, H, S, Dh), st.reshape(B, S, LANES)


def _mixer_kernel(o0_ref, o1_ref, o2_ref, s0_ref, s1_ref, s2_ref, c_ref, cp_ref, gate_ref, x_ref,
                  wap_ref, cw_ref, cb_ref, lg_ref, lb_ref, wcp_ref, wout_ref,
                  out_ref, ext_ref, conv_ref):
    tm, d_model = x_ref.shape
    conv_ch = c_ref.shape[1]
    row_blk = 64

    lses = [s0_ref[...], s1_ref[...], s2_ref[...]]
    top = jnp.maximum(jnp.maximum(lses[0], lses[1]), lses[2])
    es = [jnp.exp(s - top) for s in lses]
    inv = 1.0 / (es[0] + es[1] + es[2])
    wts = [e * inv for e in es]
    heads = []
    for h in range(HEADS_PER_GROUP):
        acc = None
        for g, o_ref in enumerate((o0_ref, o1_ref, o2_ref)):
            term = wts[g][:, h:h + 1] * o_ref[h].astype(F32)
            acc = term if acc is None else acc + term
        heads.append(acc.astype(BF16))
    y_attn = _dot(jnp.concatenate(heads, axis=1), wap_ref[...])

    halo = cp_ref[...].astype(F32)
    ext_ref[0:CONV_HALO, :] = jnp.where(pl.program_id(1) > 0, halo, 0.0)
    ext_ref[CONV_HALO:, :] = c_ref[...].astype(F32)
    first = CONV_HALO - (CONV_K - 1)

    for rb in range(tm // row_blk):
        base = rb * row_blk
        for cc in range(conv_ch // LANES):
            cs = slice(cc * LANES, (cc + 1) * LANES)
            acc = jnp.broadcast_to(cb_ref[:, cs], (row_blk, LANES))
            for k in range(CONV_K):
                lo = base + first + k
                acc = acc + ext_ref[lo:lo + row_blk, cs] * cw_ref[k:k + 1, cs]
            conv_ref[base:base + row_blk, cs] = acc
    conv = conv_ref[...]
    mu = jnp.mean(conv, axis=-1, keepdims=True)
    cen = conv - mu
    var = jnp.mean(cen * cen, axis=-1, keepdims=True)
    y = cen * lax.rsqrt(var + EPS) * lg_ref[...] + lb_ref[...]
    y_conv = _dot((y * _sigmoid(y)).astype(BF16), wcp_ref[...])

    gates = gate_ref[...].astype(F32)
    merged = gates[:, :d_model] * y_attn + gates[:, d_model:] * y_conv
    out_ref[...] = x_ref[...] + _dot(merged.astype(BF16), wout_ref[...])


def _mixer_merge(x, outs, stats, c, gates, conv_w, conv_b, ln_g, ln_b, w_attn_proj, w_conv_proj,
                 w_out, tm=512):
    B, S, D = x.shape
    conv_ch = c.shape[-1]
    H = HEADS_PER_GROUP
    tiles = S // tm
    halo_blocks = tm // CONV_HALO
    o_spec = pl.BlockSpec((None, H, tm, HEAD_DIM), lambda b, i: (b, 0, i, 0))
    s_spec = pl.BlockSpec((None, tm, LANES), lambda b, i: (b, i, 0))
    row = lambda a: a.reshape(1, -1).astype(F32)
    return pl.pallas_call(
        _mixer_kernel,
        grid=(B, tiles),
        in_specs=[o_spec, o_spec, o_spec, s_spec, s_spec, s_spec,
                  pl.BlockSpec((None, tm, conv_ch), lambda b, i: (b, i, 0)),
                  pl.BlockSpec((None, CONV_HALO, conv_ch),
                               lambda b, i: (b, jnp.maximum(i * halo_blocks - 1, 0), 0)),
                  pl.BlockSpec((tm, 2 * D), lambda b, i: (b * tiles + i, 0)),
                  pl.BlockSpec((None, tm, D), lambda b, i: (b, i, 0)),
                  _const_spec((H * HEAD_DIM, D)),
                  _const_spec((CONV_K, conv_ch)),
                  _const_spec((1, conv_ch)), _const_spec((1, conv_ch)), _const_spec((1, conv_ch)),
                  _const_spec((conv_ch, D)),
                  _const_spec((D, D))],
        out_specs=pl.BlockSpec((None, tm, D), lambda b, i: (b, i, 0)),
        out_shape=jax.ShapeDtypeStruct((B, S, D), F32),
        scratch_shapes=[pltpu.VMEM((tm + CONV_HALO, conv_ch), F32), pltpu.VMEM((tm, conv_ch), F32)],
        compiler_params=pltpu.CompilerParams(dimension_semantics=("parallel", "arbitrary"),
                                             vmem_limit_bytes=VMEM_LIMIT),
        name="mixer_merge",
    )(outs[0], outs[1], outs[2], stats[0], stats[1], stats[2], c, c, gates, x,
      w_attn_proj.astype(BF16), conv_w.astype(F32), row(conv_b), row(ln_g), row(ln_b),
      w_conv_proj.astype(BF16), w_out.astype(BF16))


def _mem_kv_kernel(m_ref, g_ref, w_ref, o_ref):
    mf = m_ref[...]
    u = (mf * _rms_scale(mf) * g_ref[...]).astype(BF16)
    o_ref[...] = _dot(u, w_ref[...]).astype(BF16)


def _cross_kernel(x_ref, g_ref, wq_ref, k_ref, v_ref, wo_ref, out_ref):
    tm, d_model = x_ref.shape
    hd = d_model // CROSS_HEADS
    xf = x_ref[...]
    u = (xf * _rms_scale(xf) * g_ref[...]).astype(BF16)
    cq = (_dot(u, wq_ref[...]) * (hd ** -0.5)).astype(BF16)
    heads = []
    for h in range(CROSS_HEADS):
        cs = slice(h * hd, (h + 1) * hd)
        s = _dot_nt(cq[:, cs], k_ref[:, cs])
        m = jnp.max(s, axis=-1, keepdims=True)
        p = jnp.exp(s - m)
        l = jnp.sum(p, axis=-1, keepdims=True)
        heads.append((_dot(p.astype(BF16), v_ref[:, cs]) * (1.0 / l)).astype(BF16))
    out_ref[...] = xf + _dot(jnp.concatenate(heads, axis=1), wo_ref[...])


def _cross_attention(x, mem, g_cross, g_mem, w_cq, w_ckv, w_co, tm=512):
    B, S, D = x.shape
    n_mem = mem.shape[1]
    params = pltpu.CompilerParams(dimension_semantics=("parallel", "arbitrary"),
                                  vmem_limit_bytes=VMEM_LIMIT)
    ckv = pl.pallas_call(
        _mem_kv_kernel,
        grid=(B, 2),
        in_specs=[pl.BlockSpec((None, n_mem, D), lambda b, j: (b, 0, 0)),
                  pl.BlockSpec((1, D), lambda b, j: (0, 0)),
                  pl.BlockSpec((D, D), lambda b, j: (0, j))],
        out_specs=pl.BlockSpec((None, n_mem, D), lambda b, j: (b, 0, j)),
        out_shape=jax.ShapeDtypeStruct((B, n_mem, 2 * D), BF16),
        compiler_params=params,
        name="mem_kv",
    )(mem, g_mem.reshape(1, D), w_ckv.astype(BF16))
    return pl.pallas_call(
        _cross_kernel,
        grid=(B, S // tm),
        in_specs=[pl.BlockSpec((None, tm, D), lambda b, i: (b, i, 0)),
                  _const_spec((1, D)),
                  _const_spec((D, D)),
                  pl.BlockSpec((None, n_mem, D), lambda b, i: (b, 0, 0)),
                  pl.BlockSpec((None, n_mem, D), lambda b, i: (b, 0, 1)),
                  _const_spec((D, D))],
        out_specs=pl.BlockSpec((None, tm, D), lambda b, i: (b, i, 0)),
        out_shape=jax.ShapeDtypeStruct((B, S, D), F32),
        compiler_params=params,
        name="cross_attn",
    )(x, g_cross.reshape(1, D), w_cq.astype(BF16), ckv, ckv, w_co.astype(BF16))


def _mlp_kernel(x_ref, g_ref, wu_ref, wd_ref, gf_ref, out_ref, *, final_norm, ff_chunk):
    xf = x_ref[...]
    u = (xf * _rms_scale(xf) * g_ref[...]).astype(BF16)
    acc = xf
    for c in range(wu_ref.shape[1] // ff_chunk):
        cs = slice(c * ff_chunk, (c + 1) * ff_chunk)
        h = jnp.maximum(_dot(u, wu_ref[:, cs]), 0.0)
        acc = acc + _dot((h * h).astype(BF16), wd_ref[cs, :])
    if final_norm:
        acc = acc * _rms_scale(acc) * gf_ref[...]
    out_ref[...] = acc


def _mlp(x, g_mlp, w_up, w_down, g_final, final_norm, tm=512, ff_chunk=1024):
    B, S, D = x.shape
    d_ff = w_up.shape[1]
    return pl.pallas_call(
        functools.partial(_mlp_kernel, final_norm=final_norm, ff_chunk=ff_chunk),
        grid=(B, S // tm),
        in_specs=[pl.BlockSpec((None, tm, D), lambda b, i: (b, i, 0)),
                  _const_spec((1, D)),
                  _const_spec((D, d_ff)),
                  _const_spec((d_ff, D)),
                  _const_spec((1, D))],
        out_specs=pl.BlockSpec((None, tm, D), lambda b, i: (b, i, 0)),
        out_shape=jax.ShapeDtypeStruct((B, S, D), F32),
        compiler_params=pltpu.CompilerParams(dimension_semantics=("parallel", "arbitrary"),
                                             vmem_limit_bytes=VMEM_LIMIT),
        name="mlp",
    )(x, g_mlp.reshape(1, D), w_up.astype(BF16), w_down.astype(BF16), g_final.reshape(1, D))


def kernel(x, mem, g_mix, w_in, b_gate, conv_w, conv_b, conv_ln_g, conv_ln_b, w_attn_proj,
           w_conv_proj, w_out, g_cross, g_mem, w_cq, w_ckv, w_co, g_mlp, w_up, w_down, g_final):
    depth = w_in.shape[0]
    for l in range(depth):
        qkv, c, gates = _in_projection(x, g_mix[l], w_in[l], b_gate[l])
        outs, stats = [], []
        for g, (_, dil) in enumerate(DIL_GROUPS):
            o_g, st_g = _dilated_attention(qkv, g, dil)
            outs.append(o_g)
            stats.append(st_g)
        x = _mixer_merge(x, outs, stats, c, gates, conv_w[l], conv_b[l], conv_ln_g[l],
                         conv_ln_b[l], w_attn_proj[l], w_conv_proj[l], w_out[l])
        x = _cross_attention(x, mem, g_cross[l], g_mem[l], w_cq[l], w_ckv[l], w_co[l])
        x = _mlp(x, g_mlp[l], w_up[l], w_down[l], g_final, final_norm=(l == depth - 1))
    return x
```

```python
import functools

import jax
import jax.numpy as jnp
from jax import lax
from jax.experimental import pallas as pl
from jax.experimental.pallas import tpu as pltpu

F32 = jnp.float32
BF16 = jnp.bfloat16

HEAD_DIM = 128
HEADS_PER_GROUP = 4
DIL_GROUPS = ((128, 1), (512, 4), (2048, 16))
N_GROUPS = len(DIL_GROUPS)
GROUP_WIDTH = HEADS_PER_GROUP * HEAD_DIM
ROT_DIM = HEAD_DIM // 4
ROPE_THETA = 500000.0
CONV_K = 31
CROSS_HEADS = 4
EPS = 1e-6

LANES = 128
BAND = 128
CONV_HALO = 32
MASK_VALUE = -1e30
VMEM_LIMIT = 56 * 1024 * 1024

for _win, _dil in DIL_GROUPS:
    assert _win // _dil == BAND


def _const_spec(shape):
    return pl.BlockSpec(shape, lambda *_: (0,) * len(shape), pipeline_mode=pl.Buffered(1))


def _rms_scale(xf):
    return lax.rsqrt(jnp.mean(xf * xf, axis=-1, keepdims=True) + EPS)


def _sigmoid(z):
    return 1.0 / (1.0 + jnp.exp(-z))


def _dot(a, b):
    return jnp.dot(a, b, preferred_element_type=F32)


def _dot_nt(a, b):
    return lax.dot_general(a, b, (((1,), (1,)), ((), ())), preferred_element_type=F32)


DEINTERLEAVE_SLABS = 8


def _in_proj_kernel(x_ref, g_ref, wqkv_ref, wglu_ref, wgate_ref, bgate_ref, tab_ref,
                    qkv0_ref, qkv1_ref, qkv2_ref, c_ref, gate_ref, slab_ref):
    tm = x_ref.shape[0]
    xf = x_ref[...]
    u = (xf * _rms_scale(xf) * g_ref[...]).astype(BF16)
    cos, sin_hi, sin_lo = tab_ref[0], tab_ref[1], tab_ref[2]
    slot = 0
    for t in range(3):
        for g, out_ref in enumerate((qkv0_ref, qkv1_ref, qkv2_ref)):
            dil = DIL_GROUPS[g][1]
            col = (t * N_GROUPS + g) * GROUP_WIDTH
            z = _dot(u, wqkv_ref[:, col:col + GROUP_WIDTH])
            for h in range(HEADS_PER_GROUP):
                zh = z[:, h * HEAD_DIM:(h + 1) * HEAD_DIM]
                if t < 2:
                    up = pltpu.roll(zh, ROT_DIM // 2, 1)
                    down = pltpu.roll(zh, HEAD_DIM - ROT_DIM // 2, 1)
                    zh = zh * cos + up * sin_hi + down * sin_lo
                if t == 0:
                    zh = zh * (HEAD_DIM ** -0.5)
                if dil == 1:
                    out_ref[t, h, 0] = zh.astype(BF16)
                else:
                    slab = slab_ref.at[slot % DEINTERLEAVE_SLABS]
                    slot += 1
                    slab[...] = zh
                    for r in range(dil):
                        out_ref[t, h, r] = slab[pl.ds(r, tm // dil, stride=dil), :].astype(BF16)
    z = _dot(u, wglu_ref[...])
    half = z.shape[1] // 2
    c_ref[...] = (z[:, :half] * _sigmoid(z[:, half:])).astype(BF16)
    z = _dot(u, wgate_ref[...])
    gate_ref[...] = _sigmoid(z + bgate_ref[...]).astype(BF16)


def _rope_tables(seq):
    half = ROT_DIM // 2
    pos = jnp.arange(seq, dtype=F32)
    inv_freq = ROPE_THETA ** (-jnp.arange(0, ROT_DIM, 2, dtype=F32) / ROT_DIM)
    ang = pos[:, None] * inv_freq[None, :]
    cos, sin = jnp.cos(ang), jnp.sin(ang)
    ones = jnp.ones((seq, HEAD_DIM - ROT_DIM), F32)
    zeros = jnp.zeros((seq, HEAD_DIM - ROT_DIM), F32)
    zhalf = jnp.zeros((seq, half), F32)
    c = jnp.concatenate([cos, cos, ones], axis=1)
    s_hi = jnp.concatenate([zhalf, sin, zeros], axis=1)
    s_lo = jnp.concatenate([-sin, zhalf, zeros], axis=1)
    return jnp.stack([c, s_hi, s_lo])


def _in_projection(x, g_mix, w_in, b_gate, tm=512):
    B, S, D = x.shape
    T = B * S
    attn_w = N_GROUPS * GROUP_WIDTH
    conv_ch = (w_in.shape[1] - 3 * attn_w - 2 * D) // 2
    wb = w_in.astype(BF16)
    w_qkv = wb[:, :3 * attn_w]
    w_glu = wb[:, 3 * attn_w:3 * attn_w + 2 * conv_ch]
    w_gate = wb[:, 3 * attn_w + 2 * conv_ch:]
    tiles_per_seq = S // tm
    H = HEADS_PER_GROUP

    def qkv_spec(dil):
        return pl.BlockSpec((3, None, H, dil, tm // dil, HEAD_DIM),
                            lambda i: (0, i // tiles_per_seq, 0, 0, i % tiles_per_seq, 0))

    def qkv_shape(dil):
        return jax.ShapeDtypeStruct((3, B, H, dil, S // dil, HEAD_DIM), BF16)

    dils = [d for _, d in DIL_GROUPS]
    outs = pl.pallas_call(
        _in_proj_kernel,
        grid=(T // tm,),
        in_specs=[pl.BlockSpec((tm, D), lambda i: (i, 0)),
                  _const_spec((1, D)),
                  _const_spec(w_qkv.shape), _const_spec(w_glu.shape), _const_spec(w_gate.shape),
                  _const_spec((1, 2 * D)),
                  pl.BlockSpec((3, tm, HEAD_DIM), lambda i: (0, i % tiles_per_seq, 0))],
        out_specs=[qkv_spec(d) for d in dils] + [
            pl.BlockSpec((tm, conv_ch), lambda i: (i, 0)),
            pl.BlockSpec((tm, 2 * D), lambda i: (i, 0))],
        out_shape=[qkv_shape(d) for d in dils] + [
            jax.ShapeDtypeStruct((T, conv_ch), BF16),
            jax.ShapeDtypeStruct((T, 2 * D), BF16)],
        scratch_shapes=[pltpu.VMEM((DEINTERLEAVE_SLABS, tm, HEAD_DIM), F32)],
        compiler_params=pltpu.CompilerParams(dimension_semantics=("arbitrary",),
                                             vmem_limit_bytes=VMEM_LIMIT),
        name="in_proj",
    )(x.reshape(T, D), g_mix.reshape(1, D), w_qkv, w_glu, w_gate, b_gate.reshape(1, 2 * D),
      _rope_tables(S))
    return outs[:N_GROUPS], outs[N_GROUPS].reshape(B, S, conv_ch), outs[N_GROUPS + 1]


def _attn_kernel(q_ref, k_ref, kp_ref, v_ref, vp_ref, o_ref, st_ref, *scratch, dil):
    rows = q_ref.shape[2]
    n_chunks = rows // BAND
    tile = pl.program_id(1)
    r_id = lax.broadcasted_iota(jnp.int32, (BAND, 2 * BAND), 0)
    c_id = lax.broadcasted_iota(jnp.int32, (BAND, 2 * BAND), 1)
    band = jnp.logical_and(c_id >= r_id, c_id <= r_id + BAND)
    band_first = jnp.logical_and(band, jnp.logical_or(c_id >= BAND, tile > 0))
    lane = lax.broadcasted_iota(jnp.int32, (BAND, LANES), 1)

    def softmax_pv(q, k, v, mask):
        s = jnp.where(mask, _dot_nt(q, k), MASK_VALUE)
        m = jnp.max(s, axis=-1, keepdims=True)
        p = jnp.exp(s - m)
        l = jnp.sum(p, axis=-1, keepdims=True)
        acc = _dot(p.astype(BF16), v)
        return acc * (1.0 / l), m + jnp.log(l)

    def block(r, n, first):
        stats = jnp.zeros((BAND, LANES), F32)
        for h in range(HEADS_PER_GROUP):
            if first:
                q = q_ref[h, r, 0:BAND, :]
                k = jnp.concatenate([kp_ref[h, r], k_ref[h, r, 0:BAND, :]], axis=0)
                v = jnp.concatenate([vp_ref[h, r], v_ref[h, r, 0:BAND, :]], axis=0)
                o, lse = softmax_pv(q, k, v, band_first)
            else:
                q = q_ref[h, r, pl.ds(pl.multiple_of(n * BAND, BAND), BAND), :]
                ks = pl.ds(pl.multiple_of((n - 1) * BAND, BAND), 2 * BAND)
                o, lse = softmax_pv(q, k_ref[h, r, ks, :], v_ref[h, r, ks, :], band)
            stats = jnp.where(lane == h, lse, stats)
            if dil == 1:
                o_ref[h, pl.ds(pl.multiple_of(n * BAND, BAND), BAND), :] = o.astype(BF16)
            else:
                scratch[0][h, pl.ds(n * (BAND * dil) + r, BAND, stride=dil), :] = o
        if dil == 1:
            st_ref[pl.ds(pl.multiple_of(n * BAND, BAND), BAND), :] = stats
        else:
            st_ref[pl.ds(n * (BAND * dil) + r, BAND, stride=dil), :] = stats

    def residue(r, carry):
        block(r, 0, True)
        if n_chunks > 1:
            def chunk(n, c):
                block(r, n, False)
                return c
            lax.fori_loop(1, n_chunks, chunk, 0)
        return carry

    if dil == 1:
        residue(0, 0)
    else:
        lax.fori_loop(0, dil, residue, 0)
        for h in range(HEADS_PER_GROUP):
            o_ref[h] = scratch[0][h].astype(BF16)


def _dilated_attention(qkv, dil, tile_tokens=2048):
    _, B, H, _, Sd, Dh = qkv.shape
    S = Sd * dil
    rows = tile_tokens // dil
    chunks = rows // BAND

    def cur(t):
        return pl.BlockSpec((None, None, H, dil, rows, Dh), lambda b, i: (t, b, 0, 0, i, 0))

    def prev(t):
        return pl.BlockSpec((None, None, H, dil, BAND, Dh),
                            lambda b, i: (t, b, 0, 0, jnp.maximum(i * chunks - 1, 0), 0))

    scratch = [] if dil == 1 else [pltpu.VMEM((H, tile_tokens, Dh), F32)]
    return pl.pallas_call(
        functools.partial(_attn_kernel, dil=dil),
        grid=(B, S // tile_tokens),
        in_specs=[cur(0), cur(1), prev(1), cur(2), prev(2)],
        out_specs=[pl.BlockSpec((None, H, tile_tokens, Dh), lambda b, i: (b, 0, i, 0)),
                   pl.BlockSpec((None, tile_tokens, LANES), lambda b, i: (b, i, 0))],
        out_shape=[jax.ShapeDtypeStruct((B, H, S, Dh), BF16),
                   jax.ShapeDtypeStruct((B, S, LANES), F32)],
        scratch_shapes=scratch,
        compiler_params=pltpu.CompilerParams(dimension_semantics=("parallel", "arbitrary"),
                                             vmem_limit_bytes=VMEM_LIMIT),
        name=f"dilated_attn_d{dil}",
    )(qkv, qkv, qkv, qkv, qkv)


CONV_ROWS = 128


def _mixer_kernel(o0_ref, o1_ref, o2_ref, s0_ref, s1_ref, s2_ref, c_ref, cp_ref, gate_ref, x_ref,
                  wap_ref, cw_ref, cb_ref, lg_ref, lb_ref, wcp_ref, wout_ref,
                  out_ref, ext_ref, conv_ref):
    tm, d_model = x_ref.shape
    conv_ch = c_ref.shape[1]
    n_slabs = conv_ch // LANES

    lses = [s0_ref[...], s1_ref[...], s2_ref[...]]
    top = jnp.maximum(jnp.maximum(lses[0], lses[1]), lses[2])
    es = [jnp.exp(s - top) for s in lses]
    inv = 1.0 / (es[0] + es[1] + es[2])
    wts = [e * inv for e in es]
    heads = []
    for h in range(HEADS_PER_GROUP):
        acc = None
        for g, o_ref in enumerate((o0_ref, o1_ref, o2_ref)):
            term = wts[g][:, h:h + 1] * o_ref[h].astype(F32)
            acc = term if acc is None else acc + term
        heads.append(acc.astype(BF16))
    y_attn = _dot(jnp.concatenate(heads, axis=1), wap_ref[...])

    halo = jnp.where(pl.program_id(1) > 0, cp_ref[...].astype(F32), 0.0)
    cur = c_ref[...].astype(F32)
    for cc in range(n_slabs):
        cs = slice(cc * LANES, (cc + 1) * LANES)
        ext_ref[cc, 0:CONV_HALO, :] = halo[:, cs]
        ext_ref[cc, CONV_HALO:, :] = cur[:, cs]
    first = CONV_HALO - (CONV_K - 1)
    half = CONV_ROWS // 2

    def conv_rows(rb, carry):
        base = rb * CONV_ROWS
        for cc in range(n_slabs):
            cs = slice(cc * LANES, (cc + 1) * LANES)
            bias = jnp.broadcast_to(cb_ref[:, cs], (half, LANES))
            even, odd = bias, bias
            for j in range(CONV_K + 1):
                rows = ext_ref[cc, pl.ds(base + first + j, half, stride=2), :]
                if j < CONV_K:
                    even = even + rows * cw_ref[j:j + 1, cs]
                if j > 0:
                    odd = odd + rows * cw_ref[j - 1:j, cs]
            conv_ref[cc, pl.ds(base, half, stride=2), :] = even
            conv_ref[cc, pl.ds(base + 1, half, stride=2), :] = odd
        return carry

    lax.fori_loop(0, tm // CONV_ROWS, conv_rows, 0)
    conv = jnp.concatenate([conv_ref[cc] for cc in range(n_slabs)], axis=1)
    mu = jnp.mean(conv, axis=-1, keepdims=True)
    cen = conv - mu
    var = jnp.mean(cen * cen, axis=-1, keepdims=True)
    y = cen * lax.rsqrt(var + EPS) * lg_ref[...] + lb_ref[...]
    y_conv = _dot((y * _sigmoid(y)).astype(BF16), wcp_ref[...])

    gates = gate_ref[...].astype(F32)
    merged = gates[:, :d_model] * y_attn + gates[:, d_model:] * y_conv
    out_ref[...] = x_ref[...] + _dot(merged.astype(BF16), wout_ref[...])


def _mixer_merge(x, outs, stats, c, gates, conv_w, conv_b, ln_g, ln_b, w_attn_proj, w_conv_proj,
                 w_out, tm=512):
    B, S, D = x.shape
    conv_ch = c.shape[-1]
    H = HEADS_PER_GROUP
    tiles = S // tm
    halo_blocks = tm // CONV_HALO
    o_spec = pl.BlockSpec((None, H, tm, HEAD_DIM), lambda b, i: (b, 0, i, 0))
    s_spec = pl.BlockSpec((None, tm, LANES), lambda b, i: (b, i, 0))
    row = lambda a: a.reshape(1, -1).astype(F32)
    return pl.pallas_call(
        _mixer_kernel,
        grid=(B, tiles),
        in_specs=[o_spec, o_spec, o_spec, s_spec, s_spec, s_spec,
                  pl.BlockSpec((None, tm, conv_ch), lambda b, i: (b, i, 0)),
                  pl.BlockSpec((None, CONV_HALO, conv_ch),
                               lambda b, i: (b, jnp.maximum(i * halo_blocks - 1, 0), 0)),
                  pl.BlockSpec((tm, 2 * D), lambda b, i: (b * tiles + i, 0)),
                  pl.BlockSpec((None, tm, D), lambda b, i: (b, i, 0)),
                  _const_spec((H * HEAD_DIM, D)),
                  _const_spec((CONV_K, conv_ch)),
                  _const_spec((1, conv_ch)), _const_spec((1, conv_ch)), _const_spec((1, conv_ch)),
                  _const_spec((conv_ch, D)),
                  _const_spec((D, D))],
        out_specs=pl.BlockSpec((None, tm, D), lambda b, i: (b, i, 0)),
        out_shape=jax.ShapeDtypeStruct((B, S, D), F32),
        scratch_shapes=[pltpu.VMEM((conv_ch // LANES, tm + CONV_HALO, LANES), F32),
                        pltpu.VMEM((conv_ch // LANES, tm, LANES), F32)],
        compiler_params=pltpu.CompilerParams(dimension_semantics=("parallel", "arbitrary"),
                                             vmem_limit_bytes=VMEM_LIMIT),
        name="mixer_merge",
    )(outs[0], outs[1], outs[2], stats[0], stats[1], stats[2], c, c, gates, x,
      w_attn_proj.astype(BF16), conv_w.astype(F32), row(conv_b), row(ln_g), row(ln_b),
      w_conv_proj.astype(BF16), w_out.astype(BF16))


def _mem_kv_kernel(m_ref, g_ref, w_ref, o_ref):
    mf = m_ref[...]
    u = (mf * _rms_scale(mf) * g_ref[...]).astype(BF16)
    o_ref[...] = _dot(u, w_ref[...]).astype(BF16)


def _cross_kernel(x_ref, g_ref, wq_ref, k_ref, v_ref, wo_ref, out_ref):
    tm, d_model = x_ref.shape
    hd = d_model // CROSS_HEADS
    xf = x_ref[...]
    u = (xf * _rms_scale(xf) * g_ref[...]).astype(BF16)
    cq = (_dot(u, wq_ref[...]) * (hd ** -0.5)).astype(BF16)
    heads = []
    for h in range(CROSS_HEADS):
        cs = slice(h * hd, (h + 1) * hd)
        s = _dot_nt(cq[:, cs], k_ref[:, cs])
        m = jnp.max(s, axis=-1, keepdims=True)
        p = jnp.exp(s - m)
        l = jnp.sum(p, axis=-1, keepdims=True)
        heads.append((_dot(p.astype(BF16), v_ref[:, cs]) * (1.0 / l)).astype(BF16))
    out_ref[...] = xf + _dot(jnp.concatenate(heads, axis=1), wo_ref[...])


def _cross_attention(x, mem, g_cross, g_mem, w_cq, w_ckv, w_co, tm=512):
    B, S, D = x.shape
    n_mem = mem.shape[1]
    params = pltpu.CompilerParams(dimension_semantics=("parallel", "arbitrary"),
                                  vmem_limit_bytes=VMEM_LIMIT)
    ckv = pl.pallas_call(
        _mem_kv_kernel,
        grid=(B, 2),
        in_specs=[pl.BlockSpec((None, n_mem, D), lambda b, j: (b, 0, 0)),
                  pl.BlockSpec((1, D), lambda b, j: (0, 0)),
                  pl.BlockSpec((D, D), lambda b, j: (0, j))],
        out_specs=pl.BlockSpec((None, n_mem, D), lambda b, j: (b, 0, j)),
        out_shape=jax.ShapeDtypeStruct((B, n_mem, 2 * D), BF16),
        compiler_params=params,
        name="mem_kv",
    )(mem, g_mem.reshape(1, D), w_ckv.astype(BF16))
    return pl.pallas_call(
        _cross_kernel,
        grid=(B, S // tm),
        in_specs=[pl.BlockSpec((None, tm, D), lambda b, i: (b, i, 0)),
                  _const_spec((1, D)),
                  _const_spec((D, D)),
                  pl.BlockSpec((None, n_mem, D), lambda b, i: (b, 0, 0)),
                  pl.BlockSpec((None, n_mem, D), lambda b, i: (b, 0, 1)),
                  _const_spec((D, D))],
        out_specs=pl.BlockSpec((None, tm, D), lambda b, i: (b, i, 0)),
        out_shape=jax.ShapeDtypeStruct((B, S, D), F32),
        compiler_params=params,
        name="cross_attn",
    )(x, g_cross.reshape(1, D), w_cq.astype(BF16), ckv, ckv, w_co.astype(BF16))


def _mlp_kernel(x_ref, g_ref, wu_ref, wd_ref, gf_ref, out_ref, *, final_norm, ff_chunk):
    xf = x_ref[...]
    u = (xf * _rms_scale(xf) * g_ref[...]).astype(BF16)
    acc = xf
    for c in range(wu_ref.shape[1] // ff_chunk):
        cs = slice(c * ff_chunk, (c + 1) * ff_chunk)
        h = jnp.maximum(_dot(u, wu_ref[:, cs]), 0.0)
        acc = acc + _dot((h * h).astype(BF16), wd_ref[cs, :])
    if final_norm:
        acc = acc * _rms_scale(acc) * gf_ref[...]
    out_ref[...] = acc


def _mlp(x, g_mlp, w_up, w_down, g_final, final_norm, tm=512, ff_chunk=1024):
    B, S, D = x.shape
    d_ff = w_up.shape[1]
    return pl.pallas_call(
        functools.partial(_mlp_kernel, final_norm=final_norm, ff_chunk=ff_chunk),
        grid=(B, S // tm),
        in_specs=[pl.BlockSpec((None, tm, D), lambda b, i: (b, i, 0)),
                  _const_spec((1, D)),
                  _const_spec((D, d_ff)),
                  _const_spec((d_ff, D)),
                  _const_spec((1, D))],
        out_specs=pl.BlockSpec((None, tm, D), lambda b, i: (b, i, 0)),
        out_shape=jax.ShapeDtypeStruct((B, S, D), F32),
        compiler_params=pltpu.CompilerParams(dimension_semantics=("parallel", "arbitrary"),
                                             vmem_limit_bytes=VMEM_LIMIT),
        name="mlp",
    )(x, g_mlp.reshape(1, D), w_up.astype(BF16), w_down.astype(BF16), g_final.reshape(1, D))


def kernel(x, mem, g_mix, w_in, b_gate, conv_w, conv_b, conv_ln_g, conv_ln_b, w_attn_proj,
           w_conv_proj, w_out, g_cross, g_mem, w_cq, w_ckv, w_co, g_mlp, w_up, w_down, g_final):
    depth = w_in.shape[0]
    for l in range(depth):
        qkvs, c, gates = _in_projection(x, g_mix[l], w_in[l], b_gate[l])
        outs, stats = [], []
        for qkv, (_, dil) in zip(qkvs, DIL_GROUPS):
            o_g, st_g = _dilated_attention(qkv, dil)
            outs.append(o_g)
            stats.append(st_g)
        x = _mixer_merge(x, outs, stats, c, gates, conv_w[l], conv_b[l], conv_ln_g[l],
                         conv_ln_b[l], w_attn_proj[l], w_conv_proj[l], w_out[l])
        x = _cross_attention(x, mem, g_cross[l], g_mem[l], w_cq[l], w_ckv[l], w_co[l])
        x = _mlp(x, g_mlp[l], w_up[l], w_down[l], g_final, final_norm=(l == depth - 1))
    return x
```

```python
import functools

import jax
import jax.numpy as jnp
from jax import lax
from jax.experimental import pallas as pl
from jax.experimental.pallas import tpu as pltpu

F32 = jnp.float32
BF16 = jnp.bfloat16

HEAD_DIM = 128
HEADS_PER_GROUP = 4
DIL_GROUPS = ((128, 1), (512, 4), (2048, 16))
N_GROUPS = len(DIL_GROUPS)
GROUP_WIDTH = HEADS_PER_GROUP * HEAD_DIM
ROT_DIM = HEAD_DIM // 4
ROPE_THETA = 500000.0
CONV_K = 31
CROSS_HEADS = 4
EPS = 1e-6

LANES = 128
BAND = 128
CONV_HALO = 32
MASK_VALUE = -1e30
VMEM_LIMIT = 56 * 1024 * 1024

for _win, _dil in DIL_GROUPS:
    assert _win // _dil == BAND


def _const_spec(shape):
    return pl.BlockSpec(shape, lambda *_: (0,) * len(shape), pipeline_mode=pl.Buffered(1))


def _rms_scale(xf):
    return lax.rsqrt(jnp.mean(xf * xf, axis=-1, keepdims=True) + EPS)


def _sigmoid(z):
    return 1.0 / (1.0 + jnp.exp(-z))


def _dot(a, b):
    return jnp.dot(a, b, preferred_element_type=F32)


def _dot_nt(a, b):
    return lax.dot_general(a, b, (((1,), (1,)), ((), ())), preferred_element_type=F32)


DEINTERLEAVE_SLABS = 8


def _in_proj_kernel(x_ref, g_ref, w_ref, tab_ref, qkv0_ref, qkv1_ref, qkv2_ref, glu_ref,
                    slab_ref):
    tm = x_ref.shape[0]
    conv_ch = glu_ref.shape[1]
    attn_w = N_GROUPS * GROUP_WIDTH
    xf = x_ref[...]
    u = (xf * _rms_scale(xf) * g_ref[...]).astype(BF16)

    z = _dot(u, w_ref[:, 3 * attn_w:3 * attn_w + 2 * conv_ch])
    glu_ref[...] = (z[:, :conv_ch] * _sigmoid(z[:, conv_ch:])).astype(BF16)

    cos, sin_hi, sin_lo = tab_ref[0], tab_ref[1], tab_ref[2]
    slot = 0
    for t in range(3):
        for g, out_ref in enumerate((qkv0_ref, qkv1_ref, qkv2_ref)):
            dil = DIL_GROUPS[g][1]
            idx = t * N_GROUPS + g
            z = _dot(u, w_ref[:, idx * GROUP_WIDTH:(idx + 1) * GROUP_WIDTH])
            for h in range(HEADS_PER_GROUP):
                zh = z[:, h * HEAD_DIM:(h + 1) * HEAD_DIM]
                if t < 2:
                    up = pltpu.roll(zh, ROT_DIM // 2, 1)
                    down = pltpu.roll(zh, HEAD_DIM - ROT_DIM // 2, 1)
                    zh = zh * cos + up * sin_hi + down * sin_lo
                if t == 0:
                    zh = zh * (HEAD_DIM ** -0.5)
                if dil == 1:
                    out_ref[t, h, 0] = zh.astype(BF16)
                else:
                    slab = slab_ref.at[slot % DEINTERLEAVE_SLABS]
                    slot += 1
                    slab[...] = zh
                    for r in range(dil):
                        out_ref[t, h, r] = slab[pl.ds(r, tm // dil, stride=dil), :].astype(BF16)


def _rope_tables(seq):
    half = ROT_DIM // 2
    pos = jnp.arange(seq, dtype=F32)
    inv_freq = ROPE_THETA ** (-jnp.arange(0, ROT_DIM, 2, dtype=F32) / ROT_DIM)
    ang = pos[:, None] * inv_freq[None, :]
    cos, sin = jnp.cos(ang), jnp.sin(ang)
    ones = jnp.ones((seq, HEAD_DIM - ROT_DIM), F32)
    zeros = jnp.zeros((seq, HEAD_DIM - ROT_DIM), F32)
    zhalf = jnp.zeros((seq, half), F32)
    c = jnp.concatenate([cos, cos, ones], axis=1)
    s_hi = jnp.concatenate([zhalf, sin, zeros], axis=1)
    s_lo = jnp.concatenate([-sin, zhalf, zeros], axis=1)
    return jnp.stack([c, s_hi, s_lo])


def _in_projection(x, g_mix, w_bf16, conv_ch, tm=512):
    B, S, D = x.shape
    T = B * S
    attn_w = N_GROUPS * GROUP_WIDTH
    assert w_bf16.shape[1] == 3 * attn_w + 2 * conv_ch + 2 * D
    tiles_per_seq = S // tm
    H = HEADS_PER_GROUP

    def qkv_spec(dil):
        return pl.BlockSpec(
            (3, None, H, dil, tm // dil, HEAD_DIM),
            lambda i: (0, i // tiles_per_seq, 0, 0, i % tiles_per_seq, 0))

    def qkv_shape(dil):
        return jax.ShapeDtypeStruct((3, B, H, dil, S // dil, HEAD_DIM), BF16)

    dils = [d for _, d in DIL_GROUPS]
    outs = pl.pallas_call(
        _in_proj_kernel,
        grid=(T // tm,),
        in_specs=[pl.BlockSpec((tm, D), lambda i: (i, 0)),
                  _const_spec((1, D)),
                  _const_spec(w_bf16.shape),
                  pl.BlockSpec((3, tm, HEAD_DIM), lambda i: (0, i % tiles_per_seq, 0))],
        out_specs=[qkv_spec(d) for d in dils] + [pl.BlockSpec((tm, conv_ch), lambda i: (i, 0))],
        out_shape=[qkv_shape(d) for d in dils] + [jax.ShapeDtypeStruct((T, conv_ch), BF16)],
        scratch_shapes=[pltpu.VMEM((DEINTERLEAVE_SLABS, tm, HEAD_DIM), F32)],
        compiler_params=pltpu.CompilerParams(
            dimension_semantics=("parallel",), vmem_limit_bytes=VMEM_LIMIT),
        name="in_proj",
    )(x.reshape(T, D), g_mix.reshape(1, D), w_bf16, _rope_tables(S))
    return outs[:N_GROUPS], outs[N_GROUPS].reshape(B, S, conv_ch)


CONV_ROWS = 16
CONV_SLABS = 2


def _conv_task(ext_ref, conv_ref, cw_ref, cb_ref, zero_ref, cc, base, after):
    cs = slice(cc * LANES, (cc + 1) * LANES)
    half = CONV_ROWS // 2
    first = CONV_HALO - (CONV_K - 1)
    start = jnp.broadcast_to(cb_ref[:, cs], (half, LANES))
    if after is not None:
        nothing = pltpu.bitcast(after, jnp.int32) & zero_ref[...]
        start = start + pltpu.bitcast(nothing, F32)
    even, odd = start, start
    for j in range(CONV_K + 1):
        rows = ext_ref[cc, pl.ds(base + first + j, half, stride=2), :]
        if j < CONV_K:
            even = even + rows * cw_ref[j:j + 1, cs]
        if j > 0:
            odd = odd + rows * cw_ref[j - 1:j, cs]
    conv_ref[cc, pl.ds(base, half, stride=2), :] = even
    conv_ref[cc, pl.ds(base + 1, half, stride=2), :] = odd
    return odd


def _attn_kernel(q_ref, k_ref, kp_ref, v_ref, vp_ref, glu_ref, gluh_ref, cw_ref, cb_ref, zero_ref,
                 o_ref, st_ref, conv_out_ref, ext_ref, conv_ref, *scratch, dil):
    rows = q_ref.shape[2]
    n_chunks = rows // BAND
    tile_tokens = rows * dil
    tile = pl.program_id(1)
    r_id = lax.broadcasted_iota(jnp.int32, (BAND, 2 * BAND), 0)
    c_id = lax.broadcasted_iota(jnp.int32, (BAND, 2 * BAND), 1)
    band = jnp.logical_and(c_id >= r_id, c_id <= r_id + BAND)
    band_first = jnp.logical_and(band, jnp.logical_or(c_id >= BAND, tile > 0))
    lane = lax.broadcasted_iota(jnp.int32, (BAND, LANES), 1)
    st_ref[...] = jnp.zeros(st_ref.shape, F32)

    halo = jnp.where(tile > 0, gluh_ref[...].astype(F32), 0.0)
    cur = glu_ref[...].astype(F32)
    for cc in range(CONV_SLABS):
        cs = slice(cc * LANES, (cc + 1) * LANES)
        ext_ref[cc, 0:CONV_HALO, :] = halo[:, cs]
        ext_ref[cc, CONV_HALO:, :] = cur[:, cs]
    conv_rows_per_head = tile_tokens // HEADS_PER_GROUP

    def head_body(h, carry):
        for r in range(dil):
            for n in range(n_chunks):
                rs = slice(n * BAND, (n + 1) * BAND)
                q = q_ref[h, r, rs, :]
                if n == 0:
                    k = jnp.concatenate([kp_ref[h, r], k_ref[h, r, rs, :]], axis=0)
                    v = jnp.concatenate([vp_ref[h, r], v_ref[h, r, rs, :]], axis=0)
                    mask = band_first
                else:
                    ks = slice((n - 1) * BAND, (n + 1) * BAND)
                    k = k_ref[h, r, ks, :]
                    v = v_ref[h, r, ks, :]
                    mask = band
                s = jnp.where(mask, _dot_nt(q, k), MASK_VALUE)
                m = jnp.max(s, axis=-1, keepdims=True)
                p = jnp.exp(s - m)
                l = jnp.sum(p, axis=-1, keepdims=True)
                o = _dot(p.astype(BF16), v) * (1.0 / l)
                lse = m + jnp.log(l)
                if dil == 1:
                    o_ref[h, rs, :] = o.astype(BF16)
                    st_ref[rs, :] = jnp.where(lane == h, lse, st_ref[rs, :])
                else:
                    ts = pl.ds(n * BAND * dil + r, BAND, stride=dil)
                    scratch[0][h, ts, :] = o
                    st_ref[ts, :] = jnp.where(lane == h, lse, st_ref[ts, :])
        if dil > 1:
            o_ref[h] = scratch[0][h].astype(BF16)
        for cc in range(CONV_SLABS):
            after = None
            for j in range(conv_rows_per_head // CONV_ROWS):
                base = h * conv_rows_per_head + j * CONV_ROWS
                after = _conv_task(ext_ref, conv_ref, cw_ref, cb_ref, zero_ref, cc, base, after)
        return carry

    lax.fori_loop(0, HEADS_PER_GROUP, head_body, 0)
    for cc in range(CONV_SLABS):
        conv_out_ref[:, cc * LANES:(cc + 1) * LANES] = conv_ref[cc].astype(BF16)


def _dilated_attention(qkv, dil, glu, conv_w, conv_b, group, tile_tokens=2048):
    _, B, H, _, Sd, Dh = qkv.shape
    S = Sd * dil
    rows = tile_tokens // dil
    chunks = rows // BAND
    width = CONV_SLABS * LANES
    halo_blocks = tile_tokens // CONV_HALO

    def cur(t):
        return pl.BlockSpec((None, None, H, dil, rows, Dh), lambda b, i: (t, b, 0, 0, i, 0))

    def prev(t):
        return pl.BlockSpec((None, None, H, dil, BAND, Dh),
                            lambda b, i: (t, b, 0, 0, jnp.maximum(i * chunks - 1, 0), 0))

    scratch = [pltpu.VMEM((CONV_SLABS, tile_tokens + CONV_HALO, LANES), F32),
               pltpu.VMEM((CONV_SLABS, tile_tokens, LANES), F32)]
    if dil > 1:
        scratch.append(pltpu.VMEM((H, tile_tokens, Dh), F32))
    return pl.pallas_call(
        functools.partial(_attn_kernel, dil=dil),
        grid=(B, S // tile_tokens),
        in_specs=[cur(0), cur(1), prev(1), cur(2), prev(2),
                  pl.BlockSpec((None, tile_tokens, width), lambda b, i: (b, i, group)),
                  pl.BlockSpec((None, CONV_HALO, width),
                               lambda b, i: (b, jnp.maximum(i * halo_blocks - 1, 0), group)),
                  pl.BlockSpec((CONV_K, width), lambda b, i: (0, group)),
                  pl.BlockSpec((1, width), lambda b, i: (0, group)),
                  _const_spec((1, LANES))],
        out_specs=[pl.BlockSpec((None, H, tile_tokens, Dh), lambda b, i: (b, 0, i, 0)),
                   pl.BlockSpec((None, tile_tokens, LANES), lambda b, i: (b, i, 0)),
                   pl.BlockSpec((None, tile_tokens, width), lambda b, i: (b, i, 0))],
        out_shape=[jax.ShapeDtypeStruct((B, H, S, Dh), BF16),
                   jax.ShapeDtypeStruct((B, S, LANES), F32),
                   jax.ShapeDtypeStruct((B, S, width), BF16)],
        scratch_shapes=scratch,
        compiler_params=pltpu.CompilerParams(dimension_semantics=("parallel", "arbitrary"),
                                             vmem_limit_bytes=VMEM_LIMIT),
        name=f"dilated_attn_d{dil}",
    )(qkv, qkv, qkv, qkv, qkv, glu, glu, conv_w.astype(F32),
      conv_b.reshape(1, -1).astype(F32), jnp.zeros((1, LANES), jnp.int32))


def _mixer_kernel(o0_ref, o1_ref, o2_ref, s0_ref, s1_ref, s2_ref, c0_ref, c1_ref, c2_ref,
                  x_ref, g_ref, wgate_ref, bgate_ref, lg_ref, lb_ref, wap_ref, wcp_ref, wout_ref,
                  out_ref):
    d_model = x_ref.shape[1]
    xf = x_ref[...]
    u = (xf * _rms_scale(xf) * g_ref[...]).astype(BF16)
    gates = _sigmoid(_dot(u, wgate_ref[...]) + bgate_ref[...])

    lses = [s0_ref[...], s1_ref[...], s2_ref[...]]
    top = jnp.maximum(jnp.maximum(lses[0], lses[1]), lses[2])
    es = [jnp.exp(s - top) for s in lses]
    inv = 1.0 / (es[0] + es[1] + es[2])
    wts = [e * inv for e in es]
    heads = []
    for h in range(HEADS_PER_GROUP):
        acc = None
        for g, o_ref in enumerate((o0_ref, o1_ref, o2_ref)):
            term = wts[g][:, h:h + 1] * o_ref[h].astype(F32)
            acc = term if acc is None else acc + term
        heads.append(acc.astype(BF16))
    y_attn = _dot(jnp.concatenate(heads, axis=1), wap_ref[...])

    conv = jnp.concatenate([c0_ref[...], c1_ref[...], c2_ref[...]], axis=1).astype(F32)
    mu = jnp.mean(conv, axis=-1, keepdims=True)
    cen = conv - mu
    var = jnp.mean(cen * cen, axis=-1, keepdims=True)
    y = cen * lax.rsqrt(var + EPS) * lg_ref[...] + lb_ref[...]
    y_conv = _dot((y * _sigmoid(y)).astype(BF16), wcp_ref[...])

    merged = gates[:, :d_model] * y_attn + gates[:, d_model:] * y_conv
    out_ref[...] = xf + _dot(merged.astype(BF16), wout_ref[...])


def _mixer_merge(x, outs, stats, convs, g_mix, w_bf16, b_gate, ln_g, ln_b, w_attn_proj,
                 w_conv_proj, w_out, tm=512):
    B, S, D = x.shape
    conv_ch = sum(c.shape[-1] for c in convs)
    H = HEADS_PER_GROUP
    tiles = S // tm
    gate_block = w_bf16.shape[1] // (2 * D) - 1
    assert (gate_block + 1) * 2 * D == w_bf16.shape[1]
    o_spec = pl.BlockSpec((None, H, tm, HEAD_DIM), lambda b, i: (b, 0, i, 0))
    s_spec = pl.BlockSpec((None, tm, LANES), lambda b, i: (b, i, 0))
    c_specs = [pl.BlockSpec((None, tm, c.shape[-1]), lambda b, i: (b, i, 0)) for c in convs]
    row = lambda a: a.reshape(1, -1).astype(F32)
    return pl.pallas_call(
        _mixer_kernel,
        grid=(B, tiles),
        in_specs=[o_spec, o_spec, o_spec, s_spec, s_spec, s_spec] + c_specs + [
            pl.BlockSpec((None, tm, D), lambda b, i: (b, i, 0)),
            _const_spec((1, D)),
            pl.BlockSpec((D, 2 * D), lambda b, i: (0, gate_block), pipeline_mode=pl.Buffered(1)),
            _const_spec((1, 2 * D)),
            _const_spec((1, conv_ch)), _const_spec((1, conv_ch)),
            _const_spec((H * HEAD_DIM, D)),
            _const_spec((conv_ch, D)),
            _const_spec((D, D))],
        out_specs=pl.BlockSpec((None, tm, D), lambda b, i: (b, i, 0)),
        out_shape=jax.ShapeDtypeStruct((B, S, D), F32),
        compiler_params=pltpu.CompilerParams(dimension_semantics=("parallel", "arbitrary"),
                                             vmem_limit_bytes=VMEM_LIMIT),
        name="mixer_merge",
    )(outs[0], outs[1], outs[2], stats[0], stats[1], stats[2], convs[0], convs[1], convs[2], x,
      row(g_mix), w_bf16, row(b_gate), row(ln_g), row(ln_b),
      w_attn_proj.astype(BF16), w_conv_proj.astype(BF16), w_out.astype(BF16))


def _mem_kv_kernel(m_ref, g_ref, w_ref, o_ref):
    mf = m_ref[...]
    u = (mf * _rms_scale(mf) * g_ref[...]).astype(BF16)
    o_ref[...] = _dot(u, w_ref[...]).astype(BF16)


def _cross_kernel(x_ref, g_ref, wq_ref, k_ref, v_ref, wo_ref, out_ref):
    tm, d_model = x_ref.shape
    hd = d_model // CROSS_HEADS
    xf = x_ref[...]
    u = (xf * _rms_scale(xf) * g_ref[...]).astype(BF16)
    cq = (_dot(u, wq_ref[...]) * (hd ** -0.5)).astype(BF16)
    heads = []
    for h in range(CROSS_HEADS):
        cs = slice(h * hd, (h + 1) * hd)
        s = _dot_nt(cq[:, cs], k_ref[:, cs])
        m = jnp.max(s, axis=-1, keepdims=True)
        p = jnp.exp(s - m)
        l = jnp.sum(p, axis=-1, keepdims=True)
        heads.append((_dot(p.astype(BF16), v_ref[:, cs]) * (1.0 / l)).astype(BF16))
    out_ref[...] = xf + _dot(jnp.concatenate(heads, axis=1), wo_ref[...])


def _cross_attention(x, mem, g_cross, g_mem, w_cq, w_ckv, w_co, tm=512):
    B, S, D = x.shape
    n_mem = mem.shape[1]
    params = pltpu.CompilerParams(dimension_semantics=("parallel", "arbitrary"),
                                  vmem_limit_bytes=VMEM_LIMIT)
    ckv = pl.pallas_call(
        _mem_kv_kernel,
        grid=(B, 2),
        in_specs=[pl.BlockSpec((None, n_mem, D), lambda b, j: (b, 0, 0)),
                  pl.BlockSpec((1, D), lambda b, j: (0, 0)),
                  pl.BlockSpec((D, D), lambda b, j: (0, j))],
        out_specs=pl.BlockSpec((None, n_mem, D), lambda b, j: (b, 0, j)),
        out_shape=jax.ShapeDtypeStruct((B, n_mem, 2 * D), BF16),
        compiler_params=params,
        name="mem_kv",
    )(mem, g_mem.reshape(1, D), w_ckv.astype(BF16))
    return pl.pallas_call(
        _cross_kernel,
        grid=(B, S // tm),
        in_specs=[pl.BlockSpec((None, tm, D), lambda b, i: (b, i, 0)),
                  _const_spec((1, D)),
                  _const_spec((D, D)),
                  pl.BlockSpec((None, n_mem, D), lambda b, i: (b, 0, 0)),
                  pl.BlockSpec((None, n_mem, D), lambda b, i: (b, 0, 1)),
                  _const_spec((D, D))],
        out_specs=pl.BlockSpec((None, tm, D), lambda b, i: (b, i, 0)),
        out_shape=jax.ShapeDtypeStruct((B, S, D), F32),
        compiler_params=params,
        name="cross_attn",
    )(x, g_cross.reshape(1, D), w_cq.astype(BF16), ckv, ckv, w_co.astype(BF16))


def _mlp_kernel(x_ref, g_ref, wu_ref, wd_ref, gf_ref, out_ref, *, final_norm, ff_chunk):
    xf = x_ref[...]
    u = (xf * _rms_scale(xf) * g_ref[...]).astype(BF16)
    acc = xf
    for c in range(wu_ref.shape[1] // ff_chunk):
        cs = slice(c * ff_chunk, (c + 1) * ff_chunk)
        h = jnp.maximum(_dot(u, wu_ref[:, cs]), 0.0)
        acc = acc + _dot((h * h).astype(BF16), wd_ref[cs, :])
    if final_norm:
        acc = acc * _rms_scale(acc) * gf_ref[...]
    out_ref[...] = acc


def _mlp(x, g_mlp, w_up, w_down, g_final, final_norm, tm=512, ff_chunk=1024):
    B, S, D = x.shape
    d_ff = w_up.shape[1]
    return pl.pallas_call(
        functools.partial(_mlp_kernel, final_norm=final_norm, ff_chunk=ff_chunk),
        grid=(B, S // tm),
        in_specs=[pl.BlockSpec((None, tm, D), lambda b, i: (b, i, 0)),
                  _const_spec((1, D)),
                  _const_spec((D, d_ff)),
                  _const_spec((d_ff, D)),
                  _const_spec((1, D))],
        out_specs=pl.BlockSpec((None, tm, D), lambda b, i: (b, i, 0)),
        out_shape=jax.ShapeDtypeStruct((B, S, D), F32),
        compiler_params=pltpu.CompilerParams(dimension_semantics=("parallel", "arbitrary"),
                                             vmem_limit_bytes=VMEM_LIMIT),
        name="mlp",
    )(x, g_mlp.reshape(1, D), w_up.astype(BF16), w_down.astype(BF16), g_final.reshape(1, D))


def kernel(x, mem, g_mix, w_in, b_gate, conv_w, conv_b, conv_ln_g, conv_ln_b, w_attn_proj,
           w_conv_proj, w_out, g_cross, g_mem, w_cq, w_ckv, w_co, g_mlp, w_up, w_down, g_final):
    depth = w_in.shape[0]
    conv_ch = conv_w.shape[-1]
    assert conv_ch == N_GROUPS * CONV_SLABS * LANES
    for l in range(depth):
        w_bf16 = w_in[l].astype(BF16)
        qkvs, glu = _in_projection(x, g_mix[l], w_bf16, conv_ch)
        outs, stats, convs = [], [], []
        for group, (qkv, (_, dil)) in enumerate(zip(qkvs, DIL_GROUPS)):
            o_g, st_g, conv_g = _dilated_attention(qkv, dil, glu, conv_w[l], conv_b[l], group)
            outs.append(o_g)
            stats.append(st_g)
            convs.append(conv_g)
        x = _mixer_merge(x, outs, stats, convs, g_mix[l], w_bf16, b_gate[l], conv_ln_g[l],
                         conv_ln_b[l], w_attn_proj[l], w_conv_proj[l], w_out[l])
        x = _cross_attention(x, mem, g_cross[l], g_mem[l], w_cq[l], w_ckv[l], w_co[l])
        x = _mlp(x, g_mlp[l], w_up[l], w_down[l], g_final, final_norm=(l == depth - 1))
    return x
```

```python
import functools

import jax
import jax.numpy as jnp
from jax import lax
from jax.experimental import pallas as pl
from jax.experimental.pallas import tpu as pltpu

F32 = jnp.float32
BF16 = jnp.bfloat16

HEAD_DIM = 128
HEADS_PER_GROUP = 4
DIL_GROUPS = ((128, 1), (512, 4), (2048, 16))
N_GROUPS = len(DIL_GROUPS)
GROUP_WIDTH = HEADS_PER_GROUP * HEAD_DIM
ROT_DIM = HEAD_DIM // 4
ROPE_THETA = 500000.0
CONV_K = 31
CROSS_HEADS = 4
EPS = 1e-6

LANES = 128
BAND = 128
CONV_HALO = 32
MASK_VALUE = -1e30
Q_SCALE = HEAD_DIM ** -0.5
VMEM_LIMIT = 56 * 1024 * 1024

for _win, _dil in DIL_GROUPS:
    assert _win // _dil == BAND


def _const_spec(shape):
    return pl.BlockSpec(shape, lambda *_: (0,) * len(shape), pipeline_mode=pl.Buffered(1))


def _rms_scale(xf):
    return lax.rsqrt(jnp.mean(xf * xf, axis=-1, keepdims=True) + EPS)


def _sigmoid(z):
    return 1.0 / (1.0 + jnp.exp(-z))


def _dot(a, b):
    return jnp.dot(a, b, preferred_element_type=F32)


def _dot_nt(a, b):
    return lax.dot_general(a, b, (((1,), (1,)), ((), ())), preferred_element_type=F32)


DEINTERLEAVE_SLABS = 8


def _in_proj_kernel(x_ref, g_ref, w_ref, tab_ref, qkv0_ref, qkv1_ref, qkv2_ref, glu_ref,
                    slab_ref):
    tm = x_ref.shape[0]
    conv_ch = glu_ref.shape[1]
    attn_w = N_GROUPS * GROUP_WIDTH
    xf = x_ref[...]
    u = (xf * _rms_scale(xf) * g_ref[...]).astype(BF16)

    z = _dot(u, w_ref[:, 3 * attn_w:3 * attn_w + 2 * conv_ch])
    glu_ref[...] = (z[:, :conv_ch] * _sigmoid(z[:, conv_ch:])).astype(BF16)

    half = ROT_DIM // 2
    tab = tab_ref[...]
    lane = lax.broadcasted_iota(jnp.int32, tab.shape, 1)
    cos = jnp.where(lane < ROT_DIM, tab, 1.0)
    sin_hi = jnp.where(jnp.logical_and(lane >= half, lane < ROT_DIM),
                       pltpu.roll(tab, HEAD_DIM - half, 1), 0.0)
    sin_lo = jnp.where(lane < half, pltpu.roll(tab, HEAD_DIM - 3 * half, 1), 0.0)
    slot = 0
    for t in range(3):
        for g, out_ref in enumerate((qkv0_ref, qkv1_ref, qkv2_ref)):
            dil = DIL_GROUPS[g][1]
            idx = t * N_GROUPS + g
            z = _dot(u, w_ref[:, idx * GROUP_WIDTH:(idx + 1) * GROUP_WIDTH])
            for h in range(HEADS_PER_GROUP):
                zh = z[:, h * HEAD_DIM:(h + 1) * HEAD_DIM]
                if t < 2:
                    up = pltpu.roll(zh, ROT_DIM // 2, 1)
                    down = pltpu.roll(zh, HEAD_DIM - ROT_DIM // 2, 1)
                    zh = zh * cos + up * sin_hi + down * sin_lo
                if t == 0:
                    zh = zh * Q_SCALE
                if dil == 1:
                    out_ref[t, h, 0] = zh.astype(BF16)
                else:
                    slab = slab_ref.at[slot % DEINTERLEAVE_SLABS]
                    slot += 1
                    slab[...] = zh
                    for r in range(dil):
                        out_ref[t, h, r] = slab[pl.ds(r, tm // dil, stride=dil), :].astype(BF16)


def _rope_table(seq):
    pos = jnp.arange(seq, dtype=F32)
    inv_freq = ROPE_THETA ** (-jnp.arange(0, ROT_DIM, 2, dtype=F32) / ROT_DIM)
    ang = pos[:, None] * inv_freq[None, :]
    cos, sin = jnp.cos(ang), jnp.sin(ang)
    zeros = jnp.zeros((seq, HEAD_DIM - 2 * ROT_DIM), F32)
    return jnp.concatenate([cos, cos, sin, -sin, zeros], axis=1)


def _in_projection(x, g_mix, w_bf16, conv_ch, tm=512):
    B, S, D = x.shape
    T = B * S
    attn_w = N_GROUPS * GROUP_WIDTH
    assert w_bf16.shape[1] == 3 * attn_w + 2 * conv_ch + 2 * D
    tiles_per_seq = S // tm
    H = HEADS_PER_GROUP

    def qkv_spec(dil):
        return pl.BlockSpec(
            (3, None, H, dil, tm // dil, HEAD_DIM),
            lambda i: (0, i // tiles_per_seq, 0, 0, i % tiles_per_seq, 0))

    def qkv_shape(dil):
        return jax.ShapeDtypeStruct((3, B, H, dil, S // dil, HEAD_DIM), BF16)

    dils = [d for _, d in DIL_GROUPS]
    outs = pl.pallas_call(
        _in_proj_kernel,
        grid=(T // tm,),
        in_specs=[pl.BlockSpec((tm, D), lambda i: (i, 0)),
                  _const_spec((1, D)),
                  _const_spec(w_bf16.shape),
                  pl.BlockSpec((tm, HEAD_DIM), lambda i: (i % tiles_per_seq, 0))],
        out_specs=[qkv_spec(d) for d in dils] + [pl.BlockSpec((tm, conv_ch), lambda i: (i, 0))],
        out_shape=[qkv_shape(d) for d in dils] + [jax.ShapeDtypeStruct((T, conv_ch), BF16)],
        scratch_shapes=[pltpu.VMEM((DEINTERLEAVE_SLABS, tm, HEAD_DIM), F32)],
        compiler_params=pltpu.CompilerParams(
            dimension_semantics=("parallel",), vmem_limit_bytes=VMEM_LIMIT),
        name="in_proj",
    )(x.reshape(T, D), g_mix.reshape(1, D), w_bf16, _rope_table(S))
    return outs[:N_GROUPS], outs[N_GROUPS].reshape(B, S, conv_ch)


CONV_ROWS = 16
CONV_SLABS = 2


def _conv_task(ext_ref, conv_ref, cw_ref, cb_ref, zero_ref, cc, base, after):
    cs = slice(cc * LANES, (cc + 1) * LANES)
    half = CONV_ROWS // 2
    first = CONV_HALO - (CONV_K - 1)
    start = jnp.broadcast_to(cb_ref[:, cs], (half, LANES))
    if after is not None:
        nothing = pltpu.bitcast(after, jnp.int32) & zero_ref[...]
        start = start + pltpu.bitcast(nothing, F32)
    even, odd = start, start
    for j in range(CONV_K + 1):
        rows = ext_ref[cc, pl.ds(base + first + j, half, stride=2), :]
        if j < CONV_K:
            even = even + rows * cw_ref[j:j + 1, cs]
        if j > 0:
            odd = odd + rows * cw_ref[j - 1:j, cs]
    conv_ref[cc, pl.ds(base, half, stride=2), :] = even
    conv_ref[cc, pl.ds(base + 1, half, stride=2), :] = odd
    return odd


def _attn_kernel(q_ref, k_ref, kp_ref, v_ref, vp_ref, glu_ref, gluh_ref, cw_ref, cb_ref, zero_ref,
                 o_ref, st_ref, conv_out_ref, ext_ref, conv_ref, *scratch, dil):
    rows = q_ref.shape[2]
    n_chunks = rows // BAND
    tile_tokens = rows * dil
    tile = pl.program_id(1)
    r_id = lax.broadcasted_iota(jnp.int32, (BAND, 2 * BAND), 0)
    c_id = lax.broadcasted_iota(jnp.int32, (BAND, 2 * BAND), 1)
    band = jnp.logical_and(c_id >= r_id, c_id <= r_id + BAND)
    band_first = jnp.logical_and(band, jnp.logical_or(c_id >= BAND, tile > 0))
    lane = lax.broadcasted_iota(jnp.int32, (BAND, LANES), 1)
    st_ref[...] = jnp.zeros(st_ref.shape, F32)

    halo = jnp.where(tile > 0, gluh_ref[...].astype(F32), 0.0)
    cur = glu_ref[...].astype(F32)
    for cc in range(CONV_SLABS):
        cs = slice(cc * LANES, (cc + 1) * LANES)
        ext_ref[cc, 0:CONV_HALO, :] = halo[:, cs]
        ext_ref[cc, CONV_HALO:, :] = cur[:, cs]
    conv_rows_per_head = tile_tokens // HEADS_PER_GROUP

    def head_body(h, carry):
        for r in range(dil):
            for n in range(n_chunks):
                rs = slice(n * BAND, (n + 1) * BAND)
                q = q_ref[h, r, rs, :]
                if n == 0:
                    k = jnp.concatenate([kp_ref[h, r], k_ref[h, r, rs, :]], axis=0)
                    v = jnp.concatenate([vp_ref[h, r], v_ref[h, r, rs, :]], axis=0)
                    mask = band_first
                else:
                    ks = slice((n - 1) * BAND, (n + 1) * BAND)
                    k = k_ref[h, r, ks, :]
                    v = v_ref[h, r, ks, :]
                    mask = band
                s = jnp.where(mask, _dot_nt(q, k), MASK_VALUE)
                m = jnp.max(s, axis=-1, keepdims=True)
                p = jnp.exp(s - m)
                l = jnp.sum(p, axis=-1, keepdims=True)
                o = _dot(p.astype(BF16), v) * (1.0 / l)
                lse = m + jnp.log(l)
                if dil == 1:
                    o_ref[h, rs, :] = o.astype(BF16)
                    st_ref[rs, :] = jnp.where(lane == h, lse, st_ref[rs, :])
                else:
                    ts = pl.ds(n * BAND * dil + r, BAND, stride=dil)
                    scratch[0][h, ts, :] = o
                    st_ref[ts, :] = jnp.where(lane == h, lse, st_ref[ts, :])
        if dil > 1:
            o_ref[h] = scratch[0][h].astype(BF16)
        for cc in range(CONV_SLABS):
            after = None
            for j in range(conv_rows_per_head // CONV_ROWS):
                base = h * conv_rows_per_head + j * CONV_ROWS
                after = _conv_task(ext_ref, conv_ref, cw_ref, cb_ref, zero_ref, cc, base, after)
        return carry

    lax.fori_loop(0, HEADS_PER_GROUP, head_body, 0)
    for cc in range(CONV_SLABS):
        conv_out_ref[:, cc * LANES:(cc + 1) * LANES] = conv_ref[cc].astype(BF16)


def _dilated_attention(qkv, dil, glu, conv_w, conv_b, group, tile_tokens=2048):
    _, B, H, _, Sd, Dh = qkv.shape
    S = Sd * dil
    rows = tile_tokens // dil
    chunks = rows // BAND
    width = CONV_SLABS * LANES
    halo_blocks = tile_tokens // CONV_HALO

    def cur(t):
        return pl.BlockSpec((None, None, H, dil, rows, Dh), lambda b, i: (t, b, 0, 0, i, 0))

    def prev(t):
        return pl.BlockSpec((None, None, H, dil, BAND, Dh),
                            lambda b, i: (t, b, 0, 0, jnp.maximum(i * chunks - 1, 0), 0))

    scratch = [pltpu.VMEM((CONV_SLABS, tile_tokens + CONV_HALO, LANES), F32),
               pltpu.VMEM((CONV_SLABS, tile_tokens, LANES), F32)]
    if dil > 1:
        scratch.append(pltpu.VMEM((H, tile_tokens, Dh), F32))
    return pl.pallas_call(
        functools.partial(_attn_kernel, dil=dil),
        grid=(B, S // tile_tokens),
        in_specs=[cur(0), cur(1), prev(1), cur(2), prev(2),
                  pl.BlockSpec((None, tile_tokens, width), lambda b, i: (b, i, group)),
                  pl.BlockSpec((None, CONV_HALO, width),
                               lambda b, i: (b, jnp.maximum(i * halo_blocks - 1, 0), group)),
                  pl.BlockSpec((CONV_K, width), lambda b, i: (0, group)),
                  pl.BlockSpec((1, width), lambda b, i: (0, group)),
                  _const_spec((1, LANES))],
        out_specs=[pl.BlockSpec((None, H, tile_tokens, Dh), lambda b, i: (b, 0, i, 0)),
                   pl.BlockSpec((None, tile_tokens, LANES), lambda b, i: (b, i, 0)),
                   pl.BlockSpec((None, tile_tokens, width), lambda b, i: (b, i, 0))],
        out_shape=[jax.ShapeDtypeStruct((B, H, S, Dh), BF16),
                   jax.ShapeDtypeStruct((B, S, LANES), F32),
                   jax.ShapeDtypeStruct((B, S, width), BF16)],
        scratch_shapes=scratch,
        compiler_params=pltpu.CompilerParams(dimension_semantics=("parallel", "arbitrary"),
                                             vmem_limit_bytes=VMEM_LIMIT),
        name=f"dilated_attn_d{dil}",
    )(qkv, qkv, qkv, qkv, qkv, glu, glu, conv_w.astype(F32),
      conv_b.reshape(1, -1).astype(F32), jnp.zeros((1, LANES), jnp.int32))


def _mixer_kernel(o0_ref, o1_ref, o2_ref, s0_ref, s1_ref, s2_ref, c0_ref, c1_ref, c2_ref,
                  x_ref, g_ref, wgate_ref, bgate_ref, lg_ref, lb_ref, wap_ref, wcp_ref, wout_ref,
                  out_ref):
    d_model = x_ref.shape[1]
    xf = x_ref[...]
    u = (xf * _rms_scale(xf) * g_ref[...]).astype(BF16)
    gates = _sigmoid(_dot(u, wgate_ref[...]) + bgate_ref[...])

    lses = [s0_ref[...], s1_ref[...], s2_ref[...]]
    top = jnp.maximum(jnp.maximum(lses[0], lses[1]), lses[2])
    es = [jnp.exp(s - top) for s in lses]
    inv = 1.0 / (es[0] + es[1] + es[2])
    wts = [e * inv for e in es]
    heads = []
    for h in range(HEADS_PER_GROUP):
        acc = None
        for g, o_ref in enumerate((o0_ref, o1_ref, o2_ref)):
            term = wts[g][:, h:h + 1] * o_ref[h].astype(F32)
            acc = term if acc is None else acc + term
        heads.append(acc.astype(BF16))
    y_attn = _dot(jnp.concatenate(heads, axis=1), wap_ref[...])

    conv = jnp.concatenate([c0_ref[...], c1_ref[...], c2_ref[...]], axis=1).astype(F32)
    mu = jnp.mean(conv, axis=-1, keepdims=True)
    cen = conv - mu
    var = jnp.mean(cen * cen, axis=-1, keepdims=True)
    y = cen * lax.rsqrt(var + EPS) * lg_ref[...] + lb_ref[...]
    y_conv = _dot((y * _sigmoid(y)).astype(BF16), wcp_ref[...])

    merged = gates[:, :d_model] * y_attn + gates[:, d_model:] * y_conv
    out_ref[...] = xf + _dot(merged.astype(BF16), wout_ref[...])


def _mixer_merge(x, outs, stats, convs, g_mix, w_bf16, b_gate, ln_g, ln_b, w_attn_proj,
                 w_conv_proj, w_out, tm=1024):
    B, S, D = x.shape
    conv_ch = sum(c.shape[-1] for c in convs)
    H = HEADS_PER_GROUP
    tiles = S // tm
    gate_block = w_bf16.shape[1] // (2 * D) - 1
    assert (gate_block + 1) * 2 * D == w_bf16.shape[1]
    o_spec = pl.BlockSpec((None, H, tm, HEAD_DIM), lambda b, i: (b, 0, i, 0))
    s_spec = pl.BlockSpec((None, tm, LANES), lambda b, i: (b, i, 0))
    c_specs = [pl.BlockSpec((None, tm, c.shape[-1]), lambda b, i: (b, i, 0)) for c in convs]
    row = lambda a: a.reshape(1, -1).astype(F32)
    return pl.pallas_call(
        _mixer_kernel,
        grid=(B, tiles),
        in_specs=[o_spec, o_spec, o_spec, s_spec, s_spec, s_spec] + c_specs + [
            pl.BlockSpec((None, tm, D), lambda b, i: (b, i, 0)),
            _const_spec((1, D)),
            pl.BlockSpec((D, 2 * D), lambda b, i: (0, gate_block), pipeline_mode=pl.Buffered(1)),
            _const_spec((1, 2 * D)),
            _const_spec((1, conv_ch)), _const_spec((1, conv_ch)),
            _const_spec((H * HEAD_DIM, D)),
            _const_spec((conv_ch, D)),
            _const_spec((D, D))],
        out_specs=pl.BlockSpec((None, tm, D), lambda b, i: (b, i, 0)),
        out_shape=jax.ShapeDtypeStruct((B, S, D), F32),
        compiler_params=pltpu.CompilerParams(dimension_semantics=("parallel", "arbitrary"),
                                             vmem_limit_bytes=VMEM_LIMIT),
        name="mixer_merge",
    )(outs[0], outs[1], outs[2], stats[0], stats[1], stats[2], convs[0], convs[1], convs[2], x,
      row(g_mix), w_bf16, row(b_gate), row(ln_g), row(ln_b),
      w_attn_proj.astype(BF16), w_conv_proj.astype(BF16), w_out.astype(BF16))


def _mem_kv_kernel(m_ref, g_ref, w_ref, o_ref):
    mf = m_ref[...]
    u = (mf * _rms_scale(mf) * g_ref[...]).astype(BF16)
    o_ref[...] = _dot(u, w_ref[...]).astype(BF16)


def _cross_kernel(x_ref, g_ref, wq_ref, k_ref, v_ref, wo_ref, out_ref):
    tm, d_model = x_ref.shape
    hd = d_model // CROSS_HEADS
    xf = x_ref[...]
    u = (xf * _rms_scale(xf) * g_ref[...]).astype(BF16)
    cq = (_dot(u, wq_ref[...]) * (hd ** -0.5)).astype(BF16)
    heads = []
    for h in range(CROSS_HEADS):
        cs = slice(h * hd, (h + 1) * hd)
        s = _dot_nt(cq[:, cs], k_ref[:, cs])
        m = jnp.max(s, axis=-1, keepdims=True)
        p = jnp.exp(s - m)
        l = jnp.sum(p, axis=-1, keepdims=True)
        heads.append((_dot(p.astype(BF16), v_ref[:, cs]) * (1.0 / l)).astype(BF16))
    out_ref[...] = xf + _dot(jnp.concatenate(heads, axis=1), wo_ref[...])


def _cross_attention(x, mem, g_cross, g_mem, w_cq, w_ckv, w_co, tm=1024):
    B, S, D = x.shape
    n_mem = mem.shape[1]
    params = pltpu.CompilerParams(dimension_semantics=("parallel", "arbitrary"),
                                  vmem_limit_bytes=VMEM_LIMIT)
    ckv = pl.pallas_call(
        _mem_kv_kernel,
        grid=(B, 2),
        in_specs=[pl.BlockSpec((None, n_mem, D), lambda b, j: (b, 0, 0)),
                  pl.BlockSpec((1, D), lambda b, j: (0, 0)),
                  pl.BlockSpec((D, D), lambda b, j: (0, j))],
        out_specs=pl.BlockSpec((None, n_mem, D), lambda b, j: (b, 0, j)),
        out_shape=jax.ShapeDtypeStruct((B, n_mem, 2 * D), BF16),
        compiler_params=params,
        name="mem_kv",
    )(mem, g_mem.reshape(1, D), w_ckv.astype(BF16))
    return pl.pallas_call(
        _cross_kernel,
        grid=(B, S // tm),
        in_specs=[pl.BlockSpec((None, tm, D), lambda b, i: (b, i, 0)),
                  _const_spec((1, D)),
                  _const_spec((D, D)),
                  pl.BlockSpec((None, n_mem, D), lambda b, i: (b, 0, 0)),
                  pl.BlockSpec((None, n_mem, D), lambda b, i: (b, 0, 1)),
                  _const_spec((D, D))],
        out_specs=pl.BlockSpec((None, tm, D), lambda b, i: (b, i, 0)),
        out_shape=jax.ShapeDtypeStruct((B, S, D), F32),
        compiler_params=params,
        name="cross_attn",
    )(x, g_cross.reshape(1, D), w_cq.astype(BF16), ckv, ckv, w_co.astype(BF16))


def _mlp_kernel(x_ref, g_ref, wu_ref, wd_ref, gf_ref, out_ref, *, final_norm, ff_chunk):
    xf = x_ref[...]
    u = (xf * _rms_scale(xf) * g_ref[...]).astype(BF16)
    acc = xf
    for c in range(wu_ref.shape[1] // ff_chunk):
        cs = slice(c * ff_chunk, (c + 1) * ff_chunk)
        h = jnp.maximum(_dot(u, wu_ref[:, cs]), 0.0)
        acc = acc + _dot((h * h).astype(BF16), wd_ref[cs, :])
    if final_norm:
        acc = acc * _rms_scale(acc) * gf_ref[...]
    out_ref[...] = acc


def _mlp(x, g_mlp, w_up, w_down, g_final, final_norm, tm=1024, ff_chunk=1024):
    B, S, D = x.shape
    d_ff = w_up.shape[1]
    return pl.pallas_call(
        functools.partial(_mlp_kernel, final_norm=final_norm, ff_chunk=ff_chunk),
        grid=(B, S // tm),
        in_specs=[pl.BlockSpec((None, tm, D), lambda b, i: (b, i, 0)),
                  _const_spec((1, D)),
                  _const_spec((D, d_ff)),
                  _const_spec((d_ff, D)),
                  _const_spec((1, D))],
        out_specs=pl.BlockSpec((None, tm, D), lambda b, i: (b, i, 0)),
        out_shape=jax.ShapeDtypeStruct((B, S, D), F32),
        compiler_params=pltpu.CompilerParams(dimension_semantics=("parallel", "arbitrary"),
                                             vmem_limit_bytes=VMEM_LIMIT),
        name="mlp",
    )(x, g_mlp.reshape(1, D), w_up.astype(BF16), w_down.astype(BF16), g_final.reshape(1, D))


def kernel(x, mem, g_mix, w_in, b_gate, conv_w, conv_b, conv_ln_g, conv_ln_b, w_attn_proj,
           w_conv_proj, w_out, g_cross, g_mem, w_cq, w_ckv, w_co, g_mlp, w_up, w_down, g_final):
    depth = w_in.shape[0]
    conv_ch = conv_w.shape[-1]
    assert conv_ch == N_GROUPS * CONV_SLABS * LANES
    for l in range(depth):
        w_bf16 = w_in[l].astype(BF16)
        qkvs, glu = _in_projection(x, g_mix[l], w_bf16, conv_ch)
        outs, stats, convs = [], [], []
        for group, (qkv, (_, dil)) in enumerate(zip(qkvs, DIL_GROUPS)):
            o_g, st_g, conv_g = _dilated_attention(qkv, dil, glu, conv_w[l], conv_b[l], group)
            outs.append(o_g)
            stats.append(st_g)
            convs.append(conv_g)
        x = _mixer_merge(x, outs, stats, convs, g_mix[l], w_bf16, b_gate[l], conv_ln_g[l],
                         conv_ln_b[l], w_attn_proj[l], w_conv_proj[l], w_out[l])
        x = _cross_attention(x, mem, g_cross[l], g_mem[l], w_cq[l], w_ckv[l], w_co[l])
        x = _mlp(x, g_mlp[l], w_up[l], w_down[l], g_final, final_norm=(l == depth - 1))
    return x
```

```python
import functools

import jax
import jax.numpy as jnp
import numpy as np
from jax import lax
from jax.experimental import pallas as pl
from jax.experimental.pallas import tpu as pltpu

F32 = jnp.float32
BF16 = jnp.bfloat16

HEAD_DIM = 128
HEADS_PER_GROUP = 4
DIL_GROUPS = ((128, 1), (512, 4), (2048, 16))
N_GROUPS = len(DIL_GROUPS)
GROUP_WIDTH = HEADS_PER_GROUP * HEAD_DIM
ROT_DIM = HEAD_DIM // 4
ROPE_THETA = 500000.0
CONV_K = 31
CROSS_HEADS = 4
EPS = 1e-6

LANES = 128
BAND = 128
CONV_HALO = 32
MASK_VALUE = -1e30
Q_SCALE = HEAD_DIM ** -0.5
VMEM_LIMIT = 56 * 1024 * 1024

for _win, _dil in DIL_GROUPS:
    assert _win // _dil == BAND


def _const_spec(shape):
    return pl.BlockSpec(shape, lambda *_: (0,) * len(shape), pipeline_mode=pl.Buffered(1))


def _rms_scale(xf):
    return lax.rsqrt(jnp.mean(xf * xf, axis=-1, keepdims=True) + EPS)


def _sigmoid(z):
    return 1.0 / (1.0 + jnp.exp(-z))


def _dot(a, b):
    return jnp.dot(a, b, preferred_element_type=F32)


def _dot_nt(a, b):
    return lax.dot_general(a, b, (((1,), (1,)), ((), ())), preferred_element_type=F32)


DEINTERLEAVE_SLABS = 8


def _in_proj_kernel(x_ref, g_ref, w_ref, tab_ref, qkv0_ref, qkv1_ref, qkv2_ref, glu_ref,
                    slab_ref):
    tm = x_ref.shape[0]
    conv_ch = glu_ref.shape[1]
    attn_w = N_GROUPS * GROUP_WIDTH
    xf = x_ref[...]
    u = (xf * _rms_scale(xf) * g_ref[...]).astype(BF16)

    z = _dot(u, w_ref[:, 3 * attn_w:3 * attn_w + 2 * conv_ch])
    glu_ref[...] = (z[:, :conv_ch] * _sigmoid(z[:, conv_ch:])).astype(BF16)

    half = ROT_DIM // 2
    tab = tab_ref[...]
    lane = lax.broadcasted_iota(jnp.int32, tab.shape, 1)
    cos = jnp.where(lane < ROT_DIM, tab, 1.0)
    sin_hi = jnp.where(jnp.logical_and(lane >= half, lane < ROT_DIM),
                       pltpu.roll(tab, HEAD_DIM - half, 1), 0.0)
    sin_lo = jnp.where(lane < half, pltpu.roll(tab, HEAD_DIM - 3 * half, 1), 0.0)
    slot = 0
    for t in range(3):
        for g, out_ref in enumerate((qkv0_ref, qkv1_ref, qkv2_ref)):
            dil = DIL_GROUPS[g][1]
            idx = t * N_GROUPS + g
            z = _dot(u, w_ref[:, idx * GROUP_WIDTH:(idx + 1) * GROUP_WIDTH])
            for h in range(HEADS_PER_GROUP):
                zh = z[:, h * HEAD_DIM:(h + 1) * HEAD_DIM]
                if t < 2:
                    up = pltpu.roll(zh, ROT_DIM // 2, 1)
                    down = pltpu.roll(zh, HEAD_DIM - ROT_DIM // 2, 1)
                    zh = zh * cos + up * sin_hi + down * sin_lo
                if t == 0:
                    zh = zh * Q_SCALE
                if dil == 1:
                    out_ref[t, h, 0] = zh.astype(BF16)
                else:
                    slab = slab_ref.at[slot % DEINTERLEAVE_SLABS]
                    slot += 1
                    slab[...] = zh
                    for r in range(dil):
                        out_ref[t, h, r] = slab[pl.ds(r, tm // dil, stride=dil), :].astype(BF16)


def _rope_table(seq):
    pos = np.arange(seq, dtype=np.float64)
    inv_freq = ROPE_THETA ** (-np.arange(0, ROT_DIM, 2, dtype=np.float64) / ROT_DIM)
    ang = pos[:, None] * inv_freq[None, :]
    cos, sin = np.cos(ang), np.sin(ang)
    zeros = np.zeros((seq, HEAD_DIM - 2 * ROT_DIM))
    return jnp.asarray(np.concatenate([cos, cos, sin, -sin, zeros], axis=1), dtype=F32)


def _in_projection(x, g_mix, w_bf16, conv_ch, tm=512):
    B, S, D = x.shape
    T = B * S
    attn_w = N_GROUPS * GROUP_WIDTH
    assert w_bf16.shape[1] == 3 * attn_w + 2 * conv_ch + 2 * D
    tiles_per_seq = S // tm
    H = HEADS_PER_GROUP

    def qkv_spec(dil):
        return pl.BlockSpec(
            (3, None, H, dil, tm // dil, HEAD_DIM),
            lambda i: (0, i // tiles_per_seq, 0, 0, i % tiles_per_seq, 0))

    def qkv_shape(dil):
        return jax.ShapeDtypeStruct((3, B, H, dil, S // dil, HEAD_DIM), BF16)

    dils = [d for _, d in DIL_GROUPS]
    outs = pl.pallas_call(
        _in_proj_kernel,
        grid=(T // tm,),
        in_specs=[pl.BlockSpec((tm, D), lambda i: (i, 0)),
                  _const_spec((1, D)),
                  _const_spec(w_bf16.shape),
                  pl.BlockSpec((tm, HEAD_DIM), lambda i: (i % tiles_per_seq, 0))],
        out_specs=[qkv_spec(d) for d in dils] + [pl.BlockSpec((tm, conv_ch), lambda i: (i, 0))],
        out_shape=[qkv_shape(d) for d in dils] + [jax.ShapeDtypeStruct((T, conv_ch), BF16)],
        scratch_shapes=[pltpu.VMEM((DEINTERLEAVE_SLABS, tm, HEAD_DIM), F32)],
        compiler_params=pltpu.CompilerParams(
            dimension_semantics=("parallel",), vmem_limit_bytes=VMEM_LIMIT),
        name="in_proj",
    )(x.reshape(T, D), g_mix.reshape(1, D), w_bf16, _rope_table(S))
    return outs[:N_GROUPS], outs[N_GROUPS].reshape(B, S, conv_ch)


CONV_ROWS = 16
CONV_SLABS = 2


def _conv_task(ext_ref, conv_ref, cw_ref, cb_ref, zero_ref, cc, base, after):
    cs = slice(cc * LANES, (cc + 1) * LANES)
    half = CONV_ROWS // 2
    first = CONV_HALO - (CONV_K - 1)
    start = jnp.broadcast_to(cb_ref[:, cs], (half, LANES))
    if after is not None:
        nothing = pltpu.bitcast(after, jnp.int32) & zero_ref[...]
        start = start + pltpu.bitcast(nothing, F32)
    even, odd = start, start
    for j in range(CONV_K + 1):
        rows = ext_ref[cc, pl.ds(base + first + j, half, stride=2), :]
        if j < CONV_K:
            even = even + rows * cw_ref[j:j + 1, cs]
        if j > 0:
            odd = odd + rows * cw_ref[j - 1:j, cs]
    conv_ref[cc, pl.ds(base, half, stride=2), :] = even
    conv_ref[cc, pl.ds(base + 1, half, stride=2), :] = odd
    return odd


def _attn_kernel(q_ref, k_ref, kp_ref, v_ref, vp_ref, glu_ref, gluh_ref, cw_ref, cb_ref, zero_ref,
                 o_ref, st_ref, conv_out_ref, ext_ref, conv_ref, *scratch, dil):
    rows = q_ref.shape[2]
    n_chunks = rows // BAND
    tile_tokens = rows * dil
    tile = pl.program_id(1)
    r_id = lax.broadcasted_iota(jnp.int32, (BAND, 2 * BAND), 0)
    c_id = lax.broadcasted_iota(jnp.int32, (BAND, 2 * BAND), 1)
    band = jnp.logical_and(c_id >= r_id, c_id <= r_id + BAND)
    band_first = jnp.logical_and(band, jnp.logical_or(c_id >= BAND, tile > 0))
    lane = lax.broadcasted_iota(jnp.int32, (BAND, LANES), 1)
    st_ref[...] = jnp.zeros(st_ref.shape, F32)

    halo = jnp.where(tile > 0, gluh_ref[...].astype(F32), 0.0)
    cur = glu_ref[...].astype(F32)
    for cc in range(CONV_SLABS):
        cs = slice(cc * LANES, (cc + 1) * LANES)
        ext_ref[cc, 0:CONV_HALO, :] = halo[:, cs]
        ext_ref[cc, CONV_HALO:, :] = cur[:, cs]
    conv_rows_per_head = tile_tokens // HEADS_PER_GROUP

    def head_body(h, carry):
        for r in range(dil):
            for n in range(n_chunks):
                rs = slice(n * BAND, (n + 1) * BAND)
                q = q_ref[h, r, rs, :]
                if n == 0:
                    k = jnp.concatenate([kp_ref[h, r], k_ref[h, r, rs, :]], axis=0)
                    v = jnp.concatenate([vp_ref[h, r], v_ref[h, r, rs, :]], axis=0)
                    mask = band_first
                else:
                    ks = slice((n - 1) * BAND, (n + 1) * BAND)
                    k = k_ref[h, r, ks, :]
                    v = v_ref[h, r, ks, :]
                    mask = band
                s = jnp.where(mask, _dot_nt(q, k), MASK_VALUE)
                m = jnp.max(s, axis=-1, keepdims=True)
                p = jnp.exp(s - m)
                l = jnp.sum(p, axis=-1, keepdims=True)
                o = _dot(p.astype(BF16), v) * (1.0 / l)
                lse = m + jnp.log(l)
                if dil == 1:
                    o_ref[h, rs, :] = o.astype(BF16)
                    st_ref[rs, :] = jnp.where(lane == h, lse, st_ref[rs, :])
                else:
                    ts = pl.ds(n * BAND * dil + r, BAND, stride=dil)
                    scratch[0][h, ts, :] = o
                    st_ref[ts, :] = jnp.where(lane == h, lse, st_ref[ts, :])
        if dil > 1:
            o_ref[h] = scratch[0][h].astype(BF16)
        for cc in range(CONV_SLABS):
            after = None
            for j in range(conv_rows_per_head // CONV_ROWS):
                base = h * conv_rows_per_head + j * CONV_ROWS
                after = _conv_task(ext_ref, conv_ref, cw_ref, cb_ref, zero_ref, cc, base, after)
        return carry

    lax.fori_loop(0, HEADS_PER_GROUP, head_body, 0)
    for cc in range(CONV_SLABS):
        conv_out_ref[:, cc * LANES:(cc + 1) * LANES] = conv_ref[cc].astype(BF16)


def _dilated_attention(qkv, dil, glu, conv_w, conv_b, group, tile_tokens=2048):
    _, B, H, _, Sd, Dh = qkv.shape
    S = Sd * dil
    rows = tile_tokens // dil
    chunks = rows // BAND
    width = CONV_SLABS * LANES
    halo_blocks = tile_tokens // CONV_HALO

    def cur(t):
        return pl.BlockSpec((None, None, H, dil, rows, Dh), lambda b, i: (t, b, 0, 0, i, 0))

    def prev(t):
        return pl.BlockSpec((None, None, H, dil, BAND, Dh),
                            lambda b, i: (t, b, 0, 0, jnp.maximum(i * chunks - 1, 0), 0))

    scratch = [pltpu.VMEM((CONV_SLABS, tile_tokens + CONV_HALO, LANES), F32),
               pltpu.VMEM((CONV_SLABS, tile_tokens, LANES), F32)]
    if dil > 1:
        scratch.append(pltpu.VMEM((H, tile_tokens, Dh), F32))
    return pl.pallas_call(
        functools.partial(_attn_kernel, dil=dil),
        grid=(B, S // tile_tokens),
        in_specs=[cur(0), cur(1), prev(1), cur(2), prev(2),
                  pl.BlockSpec((None, tile_tokens, width), lambda b, i: (b, i, group)),
                  pl.BlockSpec((None, CONV_HALO, width),
                               lambda b, i: (b, jnp.maximum(i * halo_blocks - 1, 0), group)),
                  pl.BlockSpec((CONV_K, width), lambda b, i: (0, group)),
                  pl.BlockSpec((1, width), lambda b, i: (0, group)),
                  _const_spec((1, LANES))],
        out_specs=[pl.BlockSpec((None, H, tile_tokens, Dh), lambda b, i: (b, 0, i, 0)),
                   pl.BlockSpec((None, tile_tokens, LANES), lambda b, i: (b, i, 0)),
                   pl.BlockSpec((None, tile_tokens, width), lambda b, i: (b, i, 0))],
        out_shape=[jax.ShapeDtypeStruct((B, H, S, Dh), BF16),
                   jax.ShapeDtypeStruct((B, S, LANES), F32),
                   jax.ShapeDtypeStruct((B, S, width), BF16)],
        scratch_shapes=scratch,
        compiler_params=pltpu.CompilerParams(dimension_semantics=("parallel", "arbitrary"),
                                             vmem_limit_bytes=VMEM_LIMIT),
        name=f"dilated_attn_d{dil}",
    )(qkv, qkv, qkv, qkv, qkv, glu, glu, conv_w.astype(F32),
      conv_b.reshape(1, -1).astype(F32), jnp.zeros((1, LANES), jnp.int32))


def _mixer_kernel(o0_ref, o1_ref, o2_ref, s0_ref, s1_ref, s2_ref, c0_ref, c1_ref, c2_ref,
                  x_ref, g_ref, wgate_ref, bgate_ref, lg_ref, lb_ref, wap_ref, wcp_ref, wout_ref,
                  out_ref):
    d_model = x_ref.shape[1]
    xf = x_ref[...]
    u = (xf * _rms_scale(xf) * g_ref[...]).astype(BF16)
    gates = _sigmoid(_dot(u, wgate_ref[...]) + bgate_ref[...])

    lses = [s0_ref[...], s1_ref[...], s2_ref[...]]
    top = jnp.maximum(jnp.maximum(lses[0], lses[1]), lses[2])
    es = [jnp.exp(s - top) for s in lses]
    inv = 1.0 / (es[0] + es[1] + es[2])
    wts = [e * inv for e in es]
    heads = []
    for h in range(HEADS_PER_GROUP):
        acc = None
        for g, o_ref in enumerate((o0_ref, o1_ref, o2_ref)):
            term = wts[g][:, h:h + 1] * o_ref[h].astype(F32)
            acc = term if acc is None else acc + term
        heads.append(acc.astype(BF16))
    y_attn = _dot(jnp.concatenate(heads, axis=1), wap_ref[...])

    conv = jnp.concatenate([c0_ref[...], c1_ref[...], c2_ref[...]], axis=1).astype(F32)
    mu = jnp.mean(conv, axis=-1, keepdims=True)
    cen = conv - mu
    var = jnp.mean(cen * cen, axis=-1, keepdims=True)
    y = cen * lax.rsqrt(var + EPS) * lg_ref[...] + lb_ref[...]
    y_conv = _dot((y * _sigmoid(y)).astype(BF16), wcp_ref[...])

    merged = gates[:, :d_model] * y_attn + gates[:, d_model:] * y_conv
    out_ref[...] = xf + _dot(merged.astype(BF16), wout_ref[...])


def _mixer_merge(x, outs, stats, convs, g_mix, w_bf16, b_gate, ln_g, ln_b, w_attn_proj,
                 w_conv_proj, w_out, tm=512):
    B, S, D = x.shape
    conv_ch = sum(c.shape[-1] for c in convs)
    H = HEADS_PER_GROUP
    tiles = S // tm
    gate_block = w_bf16.shape[1] // (2 * D) - 1
    assert (gate_block + 1) * 2 * D == w_bf16.shape[1]
    o_spec = pl.BlockSpec((None, H, tm, HEAD_DIM), lambda b, i: (b, 0, i, 0))
    s_spec = pl.BlockSpec((None, tm, LANES), lambda b, i: (b, i, 0))
    c_specs = [pl.BlockSpec((None, tm, c.shape[-1]), lambda b, i: (b, i, 0)) for c in convs]
    row = lambda a: a.reshape(1, -1).astype(F32)
    return pl.pallas_call(
        _mixer_kernel,
        grid=(B, tiles),
        in_specs=[o_spec, o_spec, o_spec, s_spec, s_spec, s_spec] + c_specs + [
            pl.BlockSpec((None, tm, D), lambda b, i: (b, i, 0)),
            _const_spec((1, D)),
            pl.BlockSpec((D, 2 * D), lambda b, i: (0, gate_block), pipeline_mode=pl.Buffered(1)),
            _const_spec((1, 2 * D)),
            _const_spec((1, conv_ch)), _const_spec((1, conv_ch)),
            _const_spec((H * HEAD_DIM, D)),
            _const_spec((conv_ch, D)),
            _const_spec((D, D))],
        out_specs=pl.BlockSpec((None, tm, D), lambda b, i: (b, i, 0)),
        out_shape=jax.ShapeDtypeStruct((B, S, D), F32),
        compiler_params=pltpu.CompilerParams(dimension_semantics=("parallel", "arbitrary"),
                                             vmem_limit_bytes=VMEM_LIMIT),
        name="mixer_merge",
    )(outs[0], outs[1], outs[2], stats[0], stats[1], stats[2], convs[0], convs[1], convs[2], x,
      row(g_mix), w_bf16, row(b_gate), row(ln_g), row(ln_b),
      w_attn_proj.astype(BF16), w_conv_proj.astype(BF16), w_out.astype(BF16))


def _mem_kv_kernel(m_ref, g_ref, w_ref, o_ref):
    mf = m_ref[...]
    u = (mf * _rms_scale(mf) * g_ref[...]).astype(BF16)
    o_ref[...] = _dot(u, w_ref[...]).astype(BF16)


def _cross_kernel(x_ref, g_ref, wq_ref, k_ref, v_ref, wo_ref, out_ref):
    tm, d_model = x_ref.shape
    hd = d_model // CROSS_HEADS
    xf = x_ref[...]
    u = (xf * _rms_scale(xf) * g_ref[...]).astype(BF16)
    cq = (_dot(u, wq_ref[...]) * (hd ** -0.5)).astype(BF16)
    heads = []
    for h in range(CROSS_HEADS):
        cs = slice(h * hd, (h + 1) * hd)
        s = _dot_nt(cq[:, cs], k_ref[:, cs])
        m = jnp.max(s, axis=-1, keepdims=True)
        p = jnp.exp(s - m)
        l = jnp.sum(p, axis=-1, keepdims=True)
        heads.append((_dot(p.astype(BF16), v_ref[:, cs]) * (1.0 / l)).astype(BF16))
    out_ref[...] = xf + _dot(jnp.concatenate(heads, axis=1), wo_ref[...])


def _cross_attention(x, mem, g_cross, g_mem, w_cq, w_ckv, w_co, tm=1024):
    B, S, D = x.shape
    n_mem = mem.shape[1]
    params = pltpu.CompilerParams(dimension_semantics=("parallel", "arbitrary"),
                                  vmem_limit_bytes=VMEM_LIMIT)
    ckv = pl.pallas_call(
        _mem_kv_kernel,
        grid=(B, 2),
        in_specs=[pl.BlockSpec((None, n_mem, D), lambda b, j: (b, 0, 0)),
                  pl.BlockSpec((1, D), lambda b, j: (0, 0)),
                  pl.BlockSpec((D, D), lambda b, j: (0, j))],
        out_specs=pl.BlockSpec((None, n_mem, D), lambda b, j: (b, 0, j)),
        out_shape=jax.ShapeDtypeStruct((B, n_mem, 2 * D), BF16),
        compiler_params=params,
        name="mem_kv",
    )(mem, g_mem.reshape(1, D), w_ckv.astype(BF16))
    return pl.pallas_call(
        _cross_kernel,
        grid=(B, S // tm),
        in_specs=[pl.BlockSpec((None, tm, D), lambda b, i: (b, i, 0)),
                  _const_spec((1, D)),
                  _const_spec((D, D)),
                  pl.BlockSpec((None, n_mem, D), lambda b, i: (b, 0, 0)),
                  pl.BlockSpec((None, n_mem, D), lambda b, i: (b, 0, 1)),
                  _const_spec((D, D))],
        out_specs=pl.BlockSpec((None, tm, D), lambda b, i: (b, i, 0)),
        out_shape=jax.ShapeDtypeStruct((B, S, D), F32),
        compiler_params=params,
        name="cross_attn",
    )(x, g_cross.reshape(1, D), w_cq.astype(BF16), ckv, ckv, w_co.astype(BF16))


def _mlp_kernel(x_ref, g_ref, wu_ref, wd_ref, gf_ref, out_ref, *, final_norm, ff_chunk):
    xf = x_ref[...]
    u = (xf * _rms_scale(xf) * g_ref[...]).astype(BF16)
    acc = xf
    for c in range(wu_ref.shape[1] // ff_chunk):
        cs = slice(c * ff_chunk, (c + 1) * ff_chunk)
        h = jnp.maximum(_dot(u, wu_ref[:, cs]), 0.0)
        acc = acc + _dot((h * h).astype(BF16), wd_ref[cs, :])
    if final_norm:
        acc = acc * _rms_scale(acc) * gf_ref[...]
    out_ref[...] = acc


def _mlp(x, g_mlp, w_up, w_down, g_final, final_norm, tm=1024, ff_chunk=1024):
    B, S, D = x.shape
    d_ff = w_up.shape[1]
    return pl.pallas_call(
        functools.partial(_mlp_kernel, final_norm=final_norm, ff_chunk=ff_chunk),
        grid=(B, S // tm),
        in_specs=[pl.BlockSpec((None, tm, D), lambda b, i: (b, i, 0)),
                  _const_spec((1, D)),
                  _const_spec((D, d_ff)),
                  _const_spec((d_ff, D)),
                  _const_spec((1, D))],
        out_specs=pl.BlockSpec((None, tm, D), lambda b, i: (b, i, 0)),
        out_shape=jax.ShapeDtypeStruct((B, S, D), F32),
        compiler_params=pltpu.CompilerParams(dimension_semantics=("parallel", "arbitrary"),
                                             vmem_limit_bytes=VMEM_LIMIT),
        name="mlp",
    )(x, g_mlp.reshape(1, D), w_up.astype(BF16), w_down.astype(BF16), g_final.reshape(1, D))


def kernel(x, mem, g_mix, w_in, b_gate, conv_w, conv_b, conv_ln_g, conv_ln_b, w_attn_proj,
           w_conv_proj, w_out, g_cross, g_mem, w_cq, w_ckv, w_co, g_mlp, w_up, w_down, g_final):
    depth = w_in.shape[0]
    conv_ch = conv_w.shape[-1]
    assert conv_ch == N_GROUPS * CONV_SLABS * LANES
    for l in range(depth):
        w_bf16 = w_in[l].astype(BF16)
        qkvs, glu = _in_projection(x, g_mix[l], w_bf16, conv_ch)
        outs, stats, convs = [], [], []
        for group, (qkv, (_, dil)) in enumerate(zip(qkvs, DIL_GROUPS)):
            o_g, st_g, conv_g = _dilated_attention(qkv, dil, glu, conv_w[l], conv_b[l], group)
            outs.append(o_g)
            stats.append(st_g)
            convs.append(conv_g)
        x = _mixer_merge(x, outs, stats, convs, g_mix[l], w_bf16, b_gate[l], conv_ln_g[l],
                         conv_ln_b[l], w_attn_proj[l], w_conv_proj[l], w_out[l])
        x = _cross_attention(x, mem, g_cross[l], g_mem[l], w_cq[l], w_ckv[l], w_co[l])
        x = _mlp(x, g_mlp[l], w_up[l], w_down[l], g_final, final_norm=(l == depth - 1))
    return x
```

```python
import functools

import jax
import jax.numpy as jnp
import numpy as np
from jax import lax
from jax.experimental import pallas as pl
from jax.experimental.pallas import tpu as pltpu

F32 = jnp.float32
BF16 = jnp.bfloat16

HEAD_DIM = 128
HEADS_PER_GROUP = 4
DIL_GROUPS = ((128, 1), (512, 4), (2048, 16))
N_GROUPS = len(DIL_GROUPS)
GROUP_WIDTH = HEADS_PER_GROUP * HEAD_DIM
ROT_DIM = HEAD_DIM // 4
ROPE_THETA = 500000.0
CONV_K = 31
CROSS_HEADS = 4
EPS = 1e-6

LANES = 128
BAND = 128
CONV_HALO = 32
MASK_VALUE = -1e30
Q_SCALE = HEAD_DIM ** -0.5
VMEM_LIMIT = 56 * 1024 * 1024

for _win, _dil in DIL_GROUPS:
    assert _win // _dil == BAND


def _const_spec(shape):
    return pl.BlockSpec(shape, lambda *_: (0,) * len(shape), pipeline_mode=pl.Buffered(1))


def _rms_scale(xf):
    return lax.rsqrt(jnp.mean(xf * xf, axis=-1, keepdims=True) + EPS)


def _sigmoid(z):
    return 1.0 / (1.0 + jnp.exp(-z))


def _dot(a, b):
    return jnp.dot(a, b, preferred_element_type=F32)


def _dot_nt(a, b):
    return lax.dot_general(a, b, (((1,), (1,)), ((), ())), preferred_element_type=F32)


DEINTERLEAVE_SLABS = 8


def _in_proj_kernel(x_ref, g_ref, w_ref, tab_ref, qkv0_ref, qkv1_ref, qkv2_ref, glu_ref,
                    slab_ref):
    tm = x_ref.shape[0]
    conv_ch = glu_ref.shape[1]
    attn_w = N_GROUPS * GROUP_WIDTH
    xf = x_ref[...]
    u = (xf * _rms_scale(xf) * g_ref[...]).astype(BF16)

    z = _dot(u, w_ref[:, 3 * attn_w:3 * attn_w + 2 * conv_ch])
    glu_ref[...] = (z[:, :conv_ch] * _sigmoid(z[:, conv_ch:])).astype(BF16)

    half = ROT_DIM // 2
    tab = tab_ref[...]
    lane = lax.broadcasted_iota(jnp.int32, tab.shape, 1)
    cos = jnp.where(lane < ROT_DIM, tab, 1.0)
    sin_hi = jnp.where(jnp.logical_and(lane >= half, lane < ROT_DIM),
                       pltpu.roll(tab, HEAD_DIM - half, 1), 0.0)
    sin_lo = jnp.where(lane < half, pltpu.roll(tab, HEAD_DIM - 3 * half, 1), 0.0)
    slot = 0
    for t in range(3):
        for g, out_ref in enumerate((qkv0_ref, qkv1_ref, qkv2_ref)):
            dil = DIL_GROUPS[g][1]
            idx = t * N_GROUPS + g
            z = _dot(u, w_ref[:, idx * GROUP_WIDTH:(idx + 1) * GROUP_WIDTH])
            for h in range(HEADS_PER_GROUP):
                zh = z[:, h * HEAD_DIM:(h + 1) * HEAD_DIM]
                if t < 2:
                    up = pltpu.roll(zh, ROT_DIM // 2, 1)
                    down = pltpu.roll(zh, HEAD_DIM - ROT_DIM // 2, 1)
                    zh = zh * cos + up * sin_hi + down * sin_lo
                if t == 0:
                    zh = zh * Q_SCALE
                if dil == 1:
                    out_ref[t, h, 0] = zh.astype(BF16)
                else:
                    slab = slab_ref.at[slot % DEINTERLEAVE_SLABS]
                    slot += 1
                    slab[...] = zh
                    for r in range(dil):
                        out_ref[t, h, r] = slab[pl.ds(r, tm // dil, stride=dil), :].astype(BF16)


def _rope_table(seq):
    pos = np.arange(seq, dtype=np.float64)
    inv_freq = ROPE_THETA ** (-np.arange(0, ROT_DIM, 2, dtype=np.float64) / ROT_DIM)
    ang = pos[:, None] * inv_freq[None, :]
    cos, sin = np.cos(ang), np.sin(ang)
    zeros = np.zeros((seq, HEAD_DIM - 2 * ROT_DIM))
    return jnp.asarray(np.concatenate([cos, cos, sin, -sin, zeros], axis=1), dtype=F32)


def _in_projection(x, g_mix, w_bf16, conv_ch, tm=1024):
    B, S, D = x.shape
    T = B * S
    attn_w = N_GROUPS * GROUP_WIDTH
    assert w_bf16.shape[1] == 3 * attn_w + 2 * conv_ch + 2 * D
    tiles_per_seq = S // tm
    H = HEADS_PER_GROUP

    def qkv_spec(dil):
        return pl.BlockSpec(
            (3, None, H, dil, tm // dil, HEAD_DIM),
            lambda i: (0, i // tiles_per_seq, 0, 0, i % tiles_per_seq, 0))

    def qkv_shape(dil):
        return jax.ShapeDtypeStruct((3, B, H, dil, S // dil, HEAD_DIM), BF16)

    dils = [d for _, d in DIL_GROUPS]
    outs = pl.pallas_call(
        _in_proj_kernel,
        grid=(T // tm,),
        in_specs=[pl.BlockSpec((tm, D), lambda i: (i, 0)),
                  _const_spec((1, D)),
                  _const_spec(w_bf16.shape),
                  pl.BlockSpec((tm, HEAD_DIM), lambda i: (i % tiles_per_seq, 0))],
        out_specs=[qkv_spec(d) for d in dils] + [pl.BlockSpec((tm, conv_ch), lambda i: (i, 0))],
        out_shape=[qkv_shape(d) for d in dils] + [jax.ShapeDtypeStruct((T, conv_ch), BF16)],
        scratch_shapes=[pltpu.VMEM((DEINTERLEAVE_SLABS, tm, HEAD_DIM), F32)],
        compiler_params=pltpu.CompilerParams(
            dimension_semantics=("parallel",), vmem_limit_bytes=VMEM_LIMIT),
        name="in_proj",
    )(x.reshape(T, D), g_mix.reshape(1, D), w_bf16, _rope_table(S))
    return outs[:N_GROUPS], outs[N_GROUPS].reshape(B, S, conv_ch)


CONV_ROWS = 16
CONV_SLABS = 2


def _conv_task(ext_ref, conv_ref, cw_ref, cb_ref, zero_ref, cc, base, after):
    cs = slice(cc * LANES, (cc + 1) * LANES)
    half = CONV_ROWS // 2
    first = CONV_HALO - (CONV_K - 1)
    start = jnp.broadcast_to(cb_ref[:, cs], (half, LANES))
    if after is not None:
        nothing = pltpu.bitcast(after, jnp.int32) & zero_ref[...]
        start = start + pltpu.bitcast(nothing, F32)
    even, odd = start, start
    for j in range(CONV_K + 1):
        rows = ext_ref[cc, pl.ds(base + first + j, half, stride=2), :]
        if j < CONV_K:
            even = even + rows * cw_ref[j:j + 1, cs]
        if j > 0:
            odd = odd + rows * cw_ref[j - 1:j, cs]
    conv_ref[cc, pl.ds(base, half, stride=2), :] = even
    conv_ref[cc, pl.ds(base + 1, half, stride=2), :] = odd
    return odd


def _attn_kernel(q_ref, k_ref, kp_ref, v_ref, vp_ref, glu_ref, gluh_ref, cw_ref, cb_ref, zero_ref,
                 o_ref, st_ref, conv_out_ref, ext_ref, conv_ref, *scratch, dil):
    rows = q_ref.shape[2]
    n_chunks = rows // BAND
    tile_tokens = rows * dil
    tile = pl.program_id(1)
    r_id = lax.broadcasted_iota(jnp.int32, (BAND, 2 * BAND), 0)
    c_id = lax.broadcasted_iota(jnp.int32, (BAND, 2 * BAND), 1)
    band = jnp.logical_and(c_id >= r_id, c_id <= r_id + BAND)
    band_first = jnp.logical_and(band, jnp.logical_or(c_id >= BAND, tile > 0))
    lane = lax.broadcasted_iota(jnp.int32, (BAND, LANES), 1)
    st_ref[...] = jnp.zeros(st_ref.shape, F32)

    halo = jnp.where(tile > 0, gluh_ref[...].astype(F32), 0.0)
    cur = glu_ref[...].astype(F32)
    for cc in range(CONV_SLABS):
        cs = slice(cc * LANES, (cc + 1) * LANES)
        ext_ref[cc, 0:CONV_HALO, :] = halo[:, cs]
        ext_ref[cc, CONV_HALO:, :] = cur[:, cs]
    conv_rows_per_head = tile_tokens // HEADS_PER_GROUP

    def head_body(h, carry):
        for cc in range(CONV_SLABS):
            after = None
            for j in range(conv_rows_per_head // CONV_ROWS):
                base = h * conv_rows_per_head + j * CONV_ROWS
                after = _conv_task(ext_ref, conv_ref, cw_ref, cb_ref, zero_ref, cc, base, after)
        for r in range(dil):
            for n in range(n_chunks):
                rs = slice(n * BAND, (n + 1) * BAND)
                q = q_ref[h, r, rs, :]
                if n == 0:
                    k = jnp.concatenate([kp_ref[h, r], k_ref[h, r, rs, :]], axis=0)
                    v = jnp.concatenate([vp_ref[h, r], v_ref[h, r, rs, :]], axis=0)
                    mask = band_first
                else:
                    ks = slice((n - 1) * BAND, (n + 1) * BAND)
                    k = k_ref[h, r, ks, :]
                    v = v_ref[h, r, ks, :]
                    mask = band
                s = jnp.where(mask, _dot_nt(q, k), MASK_VALUE)
                m = jnp.max(s, axis=-1, keepdims=True)
                p = jnp.exp(s - m)
                l = jnp.sum(p, axis=-1, keepdims=True)
                o = _dot(p.astype(BF16), v) * (1.0 / l)
                lse = m + jnp.log(l)
                if dil == 1:
                    o_ref[h, rs, :] = o.astype(BF16)
                    st_ref[rs, :] = jnp.where(lane == h, lse, st_ref[rs, :])
                else:
                    ts = pl.ds(n * BAND * dil + r, BAND, stride=dil)
                    scratch[0][h, ts, :] = o
                    st_ref[ts, :] = jnp.where(lane == h, lse, st_ref[ts, :])
        if dil > 1:
            o_ref[h] = scratch[0][h].astype(BF16)
        return carry

    lax.fori_loop(0, HEADS_PER_GROUP, head_body, 0)
    for cc in range(CONV_SLABS):
        conv_out_ref[:, cc * LANES:(cc + 1) * LANES] = conv_ref[cc].astype(BF16)


def _dilated_attention(qkv, dil, glu, conv_w, conv_b, group, tile_tokens=2048):
    _, B, H, _, Sd, Dh = qkv.shape
    S = Sd * dil
    rows = tile_tokens // dil
    chunks = rows // BAND
    width = CONV_SLABS * LANES
    halo_blocks = tile_tokens // CONV_HALO

    def cur(t):
        return pl.BlockSpec((None, None, H, dil, rows, Dh), lambda b, i: (t, b, 0, 0, i, 0))

    def prev(t):
        return pl.BlockSpec((None, None, H, dil, BAND, Dh),
                            lambda b, i: (t, b, 0, 0, jnp.maximum(i * chunks - 1, 0), 0))

    scratch = [pltpu.VMEM((CONV_SLABS, tile_tokens + CONV_HALO, LANES), F32),
               pltpu.VMEM((CONV_SLABS, tile_tokens, LANES), F32)]
    if dil > 1:
        scratch.append(pltpu.VMEM((H, tile_tokens, Dh), F32))
    return pl.pallas_call(
        functools.partial(_attn_kernel, dil=dil),
        grid=(B, S // tile_tokens),
        in_specs=[cur(0), cur(1), prev(1), cur(2), prev(2),
                  pl.BlockSpec((None, tile_tokens, width), lambda b, i: (b, i, group)),
                  pl.BlockSpec((None, CONV_HALO, width),
                               lambda b, i: (b, jnp.maximum(i * halo_blocks - 1, 0), group)),
                  pl.BlockSpec((CONV_K, width), lambda b, i: (0, group)),
                  pl.BlockSpec((1, width), lambda b, i: (0, group)),
                  _const_spec((1, LANES))],
        out_specs=[pl.BlockSpec((None, H, tile_tokens, Dh), lambda b, i: (b, 0, i, 0)),
                   pl.BlockSpec((None, tile_tokens, LANES), lambda b, i: (b, i, 0)),
                   pl.BlockSpec((None, tile_tokens, width), lambda b, i: (b, i, 0))],
        out_shape=[jax.ShapeDtypeStruct((B, H, S, Dh), BF16),
                   jax.ShapeDtypeStruct((B, S, LANES), F32),
                   jax.ShapeDtypeStruct((B, S, width), BF16)],
        scratch_shapes=scratch,
        compiler_params=pltpu.CompilerParams(dimension_semantics=("parallel", "arbitrary"),
                                             vmem_limit_bytes=VMEM_LIMIT),
        name=f"dilated_attn_d{dil}",
    )(qkv, qkv, qkv, qkv, qkv, glu, glu, conv_w.astype(F32),
      conv_b.reshape(1, -1).astype(F32), jnp.zeros((1, LANES), jnp.int32))


def _mixer_kernel(o0_ref, o1_ref, o2_ref, s0_ref, s1_ref, s2_ref, c0_ref, c1_ref, c2_ref,
                  x_ref, g_ref, wgate_ref, bgate_ref, lg_ref, lb_ref, wap_ref, wcp_ref, wout_ref,
                  out_ref):
    d_model = x_ref.shape[1]
    xf = x_ref[...]
    u = (xf * _rms_scale(xf) * g_ref[...]).astype(BF16)
    gates = _sigmoid(_dot(u, wgate_ref[...]) + bgate_ref[...])

    lses = [s0_ref[...], s1_ref[...], s2_ref[...]]
    top = jnp.maximum(jnp.maximum(lses[0], lses[1]), lses[2])
    es = [jnp.exp(s - top) for s in lses]
    inv = 1.0 / (es[0] + es[1] + es[2])
    wts = [e * inv for e in es]
    heads = []
    for h in range(HEADS_PER_GROUP):
        acc = None
        for g, o_ref in enumerate((o0_ref, o1_ref, o2_ref)):
            term = wts[g][:, h:h + 1] * o_ref[h].astype(F32)
            acc = term if acc is None else acc + term
        heads.append(acc.astype(BF16))
    y_attn = _dot(jnp.concatenate(heads, axis=1), wap_ref[...])

    conv = jnp.concatenate([c0_ref[...], c1_ref[...], c2_ref[...]], axis=1).astype(F32)
    mu = jnp.mean(conv, axis=-1, keepdims=True)
    cen = conv - mu
    var = jnp.mean(cen * cen, axis=-1, keepdims=True)
    y = cen * lax.rsqrt(var + EPS) * lg_ref[...] + lb_ref[...]
    y_conv = _dot((y * _sigmoid(y)).astype(BF16), wcp_ref[...])

    merged = gates[:, :d_model] * y_attn + gates[:, d_model:] * y_conv
    out_ref[...] = xf + _dot(merged.astype(BF16), wout_ref[...])


def _mixer_merge(x, outs, stats, convs, g_mix, w_bf16, b_gate, ln_g, ln_b, w_attn_proj,
                 w_conv_proj, w_out, tm=512):
    B, S, D = x.shape
    conv_ch = sum(c.shape[-1] for c in convs)
    H = HEADS_PER_GROUP
    tiles = S // tm
    gate_block = w_bf16.shape[1] // (2 * D) - 1
    assert (gate_block + 1) * 2 * D == w_bf16.shape[1]
    o_spec = pl.BlockSpec((None, H, tm, HEAD_DIM), lambda b, i: (b, 0, i, 0))
    s_spec = pl.BlockSpec((None, tm, LANES), lambda b, i: (b, i, 0))
    c_specs = [pl.BlockSpec((None, tm, c.shape[-1]), lambda b, i: (b, i, 0)) for c in convs]
    row = lambda a: a.reshape(1, -1).astype(F32)
    return pl.pallas_call(
        _mixer_kernel,
        grid=(B, tiles),
        in_specs=[o_spec, o_spec, o_spec, s_spec, s_spec, s_spec] + c_specs + [
            pl.BlockSpec((None, tm, D), lambda b, i: (b, i, 0)),
            _const_spec((1, D)),
            pl.BlockSpec((D, 2 * D), lambda b, i: (0, gate_block), pipeline_mode=pl.Buffered(1)),
            _const_spec((1, 2 * D)),
            _const_spec((1, conv_ch)), _const_spec((1, conv_ch)),
            _const_spec((H * HEAD_DIM, D)),
            _const_spec((conv_ch, D)),
            _const_spec((D, D))],
        out_specs=pl.BlockSpec((None, tm, D), lambda b, i: (b, i, 0)),
        out_shape=jax.ShapeDtypeStruct((B, S, D), F32),
        compiler_params=pltpu.CompilerParams(dimension_semantics=("parallel", "arbitrary"),
                                             vmem_limit_bytes=VMEM_LIMIT),
        name="mixer_merge",
    )(outs[0], outs[1], outs[2], stats[0], stats[1], stats[2], convs[0], convs[1], convs[2], x,
      row(g_mix), w_bf16, row(b_gate), row(ln_g), row(ln_b),
      w_attn_proj.astype(BF16), w_conv_proj.astype(BF16), w_out.astype(BF16))


def _mem_kv_kernel(m_ref, g_ref, w_ref, o_ref):
    mf = m_ref[...]
    u = (mf * _rms_scale(mf) * g_ref[...]).astype(BF16)
    o_ref[...] = _dot(u, w_ref[...]).astype(BF16)


def _cross_kernel(x_ref, g_ref, wq_ref, k_ref, v_ref, wo_ref, out_ref):
    tm, d_model = x_ref.shape
    hd = d_model // CROSS_HEADS
    xf = x_ref[...]
    u = (xf * _rms_scale(xf) * g_ref[...]).astype(BF16)
    cq = (_dot(u, wq_ref[...]) * (hd ** -0.5)).astype(BF16)
    heads = []
    for h in range(CROSS_HEADS):
        cs = slice(h * hd, (h + 1) * hd)
        s = _dot_nt(cq[:, cs], k_ref[:, cs])
        m = jnp.max(s, axis=-1, keepdims=True)
        p = jnp.exp(s - m)
        l = jnp.sum(p, axis=-1, keepdims=True)
        heads.append((_dot(p.astype(BF16), v_ref[:, cs]) * (1.0 / l)).astype(BF16))
    out_ref[...] = xf + _dot(jnp.concatenate(heads, axis=1), wo_ref[...])


def _cross_attention(x, mem, g_cross, g_mem, w_cq, w_ckv, w_co, tm=1024):
    B, S, D = x.shape
    n_mem = mem.shape[1]
    params = pltpu.CompilerParams(dimension_semantics=("parallel", "arbitrary"),
                                  vmem_limit_bytes=VMEM_LIMIT)
    ckv = pl.pallas_call(
        _mem_kv_kernel,
        grid=(B, 2),
        in_specs=[pl.BlockSpec((None, n_mem, D), lambda b, j: (b, 0, 0)),
                  pl.BlockSpec((1, D), lambda b, j: (0, 0)),
                  pl.BlockSpec((D, D), lambda b, j: (0, j))],
        out_specs=pl.BlockSpec((None, n_mem, D), lambda b, j: (b, 0, j)),
        out_shape=jax.ShapeDtypeStruct((B, n_mem, 2 * D), BF16),
        compiler_params=params,
        name="mem_kv",
    )(mem, g_mem.reshape(1, D), w_ckv.astype(BF16))
    return pl.pallas_call(
        _cross_kernel,
        grid=(B, S // tm),
        in_specs=[pl.BlockSpec((None, tm, D), lambda b, i: (b, i, 0)),
                  _const_spec((1, D)),
                  _const_spec((D, D)),
                  pl.BlockSpec((None, n_mem, D), lambda b, i: (b, 0, 0)),
                  pl.BlockSpec((None, n_mem, D), lambda b, i: (b, 0, 1)),
                  _const_spec((D, D))],
        out_specs=pl.BlockSpec((None, tm, D), lambda b, i: (b, i, 0)),
        out_shape=jax.ShapeDtypeStruct((B, S, D), F32),
        compiler_params=params,
        name="cross_attn",
    )(x, g_cross.reshape(1, D), w_cq.astype(BF16), ckv, ckv, w_co.astype(BF16))


def _mlp_kernel(x_ref, g_ref, wu_ref, wd_ref, gf_ref, out_ref, *, final_norm, ff_chunk):
    xf = x_ref[...]
    u = (xf * _rms_scale(xf) * g_ref[...]).astype(BF16)
    acc = xf
    for c in range(wu_ref.shape[1] // ff_chunk):
        cs = slice(c * ff_chunk, (c + 1) * ff_chunk)
        h = jnp.maximum(_dot(u, wu_ref[:, cs]), 0.0)
        acc = acc + _dot((h * h).astype(BF16), wd_ref[cs, :])
    if final_norm:
        acc = acc * _rms_scale(acc) * gf_ref[...]
    out_ref[...] = acc


def _mlp(x, g_mlp, w_up, w_down, g_final, final_norm, tm=1024, ff_chunk=1024):
    B, S, D = x.shape
    d_ff = w_up.shape[1]
    return pl.pallas_call(
        functools.partial(_mlp_kernel, final_norm=final_norm, ff_chunk=ff_chunk),
        grid=(B, S // tm),
        in_specs=[pl.BlockSpec((None, tm, D), lambda b, i: (b, i, 0)),
                  _const_spec((1, D)),
                  _const_spec((D, d_ff)),
                  _const_spec((d_ff, D)),
                  _const_spec((1, D))],
        out_specs=pl.BlockSpec((None, tm, D), lambda b, i: (b, i, 0)),
        out_shape=jax.ShapeDtypeStruct((B, S, D), F32),
        compiler_params=pltpu.CompilerParams(dimension_semantics=("parallel", "arbitrary"),
                                             vmem_limit_bytes=VMEM_LIMIT),
        name="mlp",
    )(x, g_mlp.reshape(1, D), w_up.astype(BF16), w_down.astype(BF16), g_final.reshape(1, D))


def kernel(x, mem, g_mix, w_in, b_gate, conv_w, conv_b, conv_ln_g, conv_ln_b, w_attn_proj,
           w_conv_proj, w_out, g_cross, g_mem, w_cq, w_ckv, w_co, g_mlp, w_up, w_down, g_final):
    depth = w_in.shape[0]
    conv_ch = conv_w.shape[-1]
    assert conv_ch == N_GROUPS * CONV_SLABS * LANES
    for l in range(depth):
        w_bf16 = w_in[l].astype(BF16)
        qkvs, glu = _in_projection(x, g_mix[l], w_bf16, conv_ch)
        outs, stats, convs = [], [], []
        for group, (qkv, (_, dil)) in enumerate(zip(qkvs, DIL_GROUPS)):
            o_g, st_g, conv_g = _dilated_attention(qkv, dil, glu, conv_w[l], conv_b[l], group)
            outs.append(o_g)
            stats.append(st_g)
            convs.append(conv_g)
        x = _mixer_merge(x, outs, stats, convs, g_mix[l], w_bf16, b_gate[l], conv_ln_g[l],
                         conv_ln_b[l], w_attn_proj[l], w_conv_proj[l], w_out[l])
        x = _cross_attention(x, mem, g_cross[l], g_mem[l], w_cq[l], w_ckv[l], w_co[l])
        x = _mlp(x, g_mlp[l], w_up[l], w_down[l], g_final, final_norm=(l == depth - 1))
    return x
```

```python
import functools

import jax
import jax.numpy as jnp
import numpy as np
from jax import lax
from jax.experimental import pallas as pl
from jax.experimental.pallas import tpu as pltpu

F32 = jnp.float32
BF16 = jnp.bfloat16

HEAD_DIM = 128
HEADS_PER_GROUP = 4
DIL_GROUPS = ((128, 1), (512, 4), (2048, 16))
N_GROUPS = len(DIL_GROUPS)
GROUP_WIDTH = HEADS_PER_GROUP * HEAD_DIM
ROT_DIM = HEAD_DIM // 4
ROPE_THETA = 500000.0
CONV_K = 31
CROSS_HEADS = 4
EPS = 1e-6

LANES = 128
BAND = 128
CONV_HALO = 32
MASK_VALUE = -1e30
Q_SCALE = HEAD_DIM ** -0.5
VMEM_LIMIT = 56 * 1024 * 1024

for _win, _dil in DIL_GROUPS:
    assert _win // _dil == BAND


def _const_spec(shape):
    return pl.BlockSpec(shape, lambda *_: (0,) * len(shape), pipeline_mode=pl.Buffered(1))


def _rms_scale(xf):
    return lax.rsqrt(jnp.mean(xf * xf, axis=-1, keepdims=True) + EPS)


def _sigmoid(z):
    return 1.0 / (1.0 + jnp.exp(-z))


def _dot(a, b):
    return jnp.dot(a, b, preferred_element_type=F32)


def _dot_nt(a, b):
    return lax.dot_general(a, b, (((1,), (1,)), ((), ())), preferred_element_type=F32)


DEINTERLEAVE_SLABS = 8


def _in_proj_kernel(x_ref, g_ref, w_ref, tab_ref, qkv0_ref, qkv1_ref, qkv2_ref, glu_ref,
                    slab_ref):
    tm = x_ref.shape[0]
    conv_ch = glu_ref.shape[1]
    attn_w = N_GROUPS * GROUP_WIDTH
    xf = x_ref[...]
    u = (xf * _rms_scale(xf) * g_ref[...]).astype(BF16)

    z = _dot(u, w_ref[:, 3 * attn_w:3 * attn_w + 2 * conv_ch])
    glu_ref[...] = (z[:, :conv_ch] * _sigmoid(z[:, conv_ch:])).astype(BF16)

    half = ROT_DIM // 2
    tab = tab_ref[...]
    lane = lax.broadcasted_iota(jnp.int32, tab.shape, 1)
    cos = jnp.where(lane < ROT_DIM, tab, 1.0)
    sin_hi = jnp.where(jnp.logical_and(lane >= half, lane < ROT_DIM),
                       pltpu.roll(tab, HEAD_DIM - half, 1), 0.0)
    sin_lo = jnp.where(lane < half, pltpu.roll(tab, HEAD_DIM - 3 * half, 1), 0.0)
    slot = 0
    for t in range(3):
        for g, out_ref in enumerate((qkv0_ref, qkv1_ref, qkv2_ref)):
            dil = DIL_GROUPS[g][1]
            idx = t * N_GROUPS + g
            z = _dot(u, w_ref[:, idx * GROUP_WIDTH:(idx + 1) * GROUP_WIDTH])
            for h in range(HEADS_PER_GROUP):
                zh = z[:, h * HEAD_DIM:(h + 1) * HEAD_DIM]
                if t < 2:
                    up = pltpu.roll(zh, ROT_DIM // 2, 1)
                    down = pltpu.roll(zh, HEAD_DIM - ROT_DIM // 2, 1)
                    zh = zh * cos + up * sin_hi + down * sin_lo
                if t == 0:
                    zh = zh * Q_SCALE
                if dil == 1:
                    out_ref[t, h, 0] = zh.astype(BF16)
                else:
                    slab = slab_ref.at[slot % DEINTERLEAVE_SLABS]
                    slot += 1
                    slab[...] = zh
                    for r in range(dil):
                        out_ref[t, h, r] = slab[pl.ds(r, tm // dil, stride=dil), :].astype(BF16)


def _rope_table(seq):
    pos = np.arange(seq, dtype=np.float64)
    inv_freq = ROPE_THETA ** (-np.arange(0, ROT_DIM, 2, dtype=np.float64) / ROT_DIM)
    ang = pos[:, None] * inv_freq[None, :]
    cos, sin = np.cos(ang), np.sin(ang)
    zeros = np.zeros((seq, HEAD_DIM - 2 * ROT_DIM))
    return jnp.asarray(np.concatenate([cos, cos, sin, -sin, zeros], axis=1), dtype=F32)


def _in_projection(x, g_mix, w_bf16, conv_ch, tm=1024):
    B, S, D = x.shape
    T = B * S
    attn_w = N_GROUPS * GROUP_WIDTH
    assert w_bf16.shape[1] == 3 * attn_w + 2 * conv_ch + 2 * D
    tiles_per_seq = S // tm
    H = HEADS_PER_GROUP

    def qkv_spec(dil):
        return pl.BlockSpec(
            (3, None, H, dil, tm // dil, HEAD_DIM),
            lambda i: (0, i // tiles_per_seq, 0, 0, i % tiles_per_seq, 0))

    def qkv_shape(dil):
        return jax.ShapeDtypeStruct((3, B, H, dil, S // dil, HEAD_DIM), BF16)

    dils = [d for _, d in DIL_GROUPS]
    outs = pl.pallas_call(
        _in_proj_kernel,
        grid=(T // tm,),
        in_specs=[pl.BlockSpec((tm, D), lambda i: (i, 0)),
                  _const_spec((1, D)),
                  _const_spec(w_bf16.shape),
                  pl.BlockSpec((tm, HEAD_DIM), lambda i: (i % tiles_per_seq, 0))],
        out_specs=[qkv_spec(d) for d in dils] + [pl.BlockSpec((tm, conv_ch), lambda i: (i, 0))],
        out_shape=[qkv_shape(d) for d in dils] + [jax.ShapeDtypeStruct((T, conv_ch), BF16)],
        scratch_shapes=[pltpu.VMEM((DEINTERLEAVE_SLABS, tm, HEAD_DIM), F32)],
        compiler_params=pltpu.CompilerParams(
            dimension_semantics=("parallel",), vmem_limit_bytes=VMEM_LIMIT),
        name="in_proj",
    )(x.reshape(T, D), g_mix.reshape(1, D), w_bf16, _rope_table(S))
    return outs[:N_GROUPS], outs[N_GROUPS].reshape(B, S, conv_ch)


CONV_ROWS = 16
CONV_SLABS = 2


def _conv_task(ext_ref, conv_ref, cw_ref, cb_ref, zero_ref, cc, base, after):
    cs = slice(cc * LANES, (cc + 1) * LANES)
    half = CONV_ROWS // 2
    first = CONV_HALO - (CONV_K - 1)
    start = jnp.broadcast_to(cb_ref[:, cs], (half, LANES))
    if after is not None:
        nothing = pltpu.bitcast(after, jnp.int32) & zero_ref[...]
        start = start + pltpu.bitcast(nothing, F32)
    even, odd = start, start
    for j in range(CONV_K + 1):
        rows = ext_ref[cc, pl.ds(base + first + j, half, stride=2), :]
        if j < CONV_K:
            even = even + rows * cw_ref[j:j + 1, cs]
        if j > 0:
            odd = odd + rows * cw_ref[j - 1:j, cs]
    conv_ref[cc, pl.ds(base, half, stride=2), :] = even
    conv_ref[cc, pl.ds(base + 1, half, stride=2), :] = odd
    return odd


def _attn_kernel(q_ref, k_ref, kp_ref, v_ref, vp_ref, glu_ref, gluh_ref, cw_ref, cb_ref, zero_ref,
                 o_ref, st_ref, conv_out_ref, ext_ref, conv_ref, *scratch, dil):
    rows = q_ref.shape[2]
    n_chunks = rows // BAND
    tile_tokens = rows * dil
    tile = pl.program_id(1)
    r_id = lax.broadcasted_iota(jnp.int32, (BAND, 2 * BAND), 0)
    c_id = lax.broadcasted_iota(jnp.int32, (BAND, 2 * BAND), 1)
    band = jnp.logical_and(c_id >= r_id, c_id <= r_id + BAND)
    band_first = jnp.logical_and(band, jnp.logical_or(c_id >= BAND, tile > 0))
    lane = lax.broadcasted_iota(jnp.int32, (BAND, LANES), 1)
    st_ref[...] = jnp.zeros(st_ref.shape, F32)

    halo = jnp.where(tile > 0, gluh_ref[...].astype(F32), 0.0)
    cur = glu_ref[...].astype(F32)
    for cc in range(CONV_SLABS):
        cs = slice(cc * LANES, (cc + 1) * LANES)
        ext_ref[cc, 0:CONV_HALO, :] = halo[:, cs]
        ext_ref[cc, CONV_HALO:, :] = cur[:, cs]
    conv_rows_per_head = tile_tokens // HEADS_PER_GROUP

    def head_body(h, carry):
        after = [None] * CONV_SLABS
        tasks_per_block = conv_rows_per_head // CONV_ROWS // (dil * n_chunks)
        for r in range(dil):
            for n in range(n_chunks):
                for j in range(tasks_per_block):
                    row = ((r * n_chunks + n) * tasks_per_block + j) * CONV_ROWS
                    for cc in range(CONV_SLABS):
                        after[cc] = _conv_task(ext_ref, conv_ref, cw_ref, cb_ref, zero_ref, cc,
                                               h * conv_rows_per_head + row, after[cc])
                rs = slice(n * BAND, (n + 1) * BAND)
                q = q_ref[h, r, rs, :]
                if n == 0:
                    k = jnp.concatenate([kp_ref[h, r], k_ref[h, r, rs, :]], axis=0)
                    v = jnp.concatenate([vp_ref[h, r], v_ref[h, r, rs, :]], axis=0)
                    mask = band_first
                else:
                    ks = slice((n - 1) * BAND, (n + 1) * BAND)
                    k = k_ref[h, r, ks, :]
                    v = v_ref[h, r, ks, :]
                    mask = band
                s = jnp.where(mask, _dot_nt(q, k), MASK_VALUE)
                m = jnp.max(s, axis=-1, keepdims=True)
                p = jnp.exp(s - m)
                l = jnp.sum(p, axis=-1, keepdims=True)
                o = _dot(p.astype(BF16), v) * (1.0 / l)
                lse = m + jnp.log(l)
                if dil == 1:
                    o_ref[h, rs, :] = o.astype(BF16)
                    st_ref[rs, :] = jnp.where(lane == h, lse, st_ref[rs, :])
                else:
                    ts = pl.ds(n * BAND * dil + r, BAND, stride=dil)
                    scratch[0][h, ts, :] = o
                    st_ref[ts, :] = jnp.where(lane == h, lse, st_ref[ts, :])
        if dil > 1:
            o_ref[h] = scratch[0][h].astype(BF16)
        return carry

    lax.fori_loop(0, HEADS_PER_GROUP, head_body, 0)
    for cc in range(CONV_SLABS):
        conv_out_ref[:, cc * LANES:(cc + 1) * LANES] = conv_ref[cc].astype(BF16)


def _dilated_attention(qkv, dil, glu, conv_w, conv_b, group, tile_tokens=2048):
    _, B, H, _, Sd, Dh = qkv.shape
    S = Sd * dil
    rows = tile_tokens // dil
    chunks = rows // BAND
    width = CONV_SLABS * LANES
    halo_blocks = tile_tokens // CONV_HALO

    def cur(t):
        return pl.BlockSpec((None, None, H, dil, rows, Dh), lambda b, i: (t, b, 0, 0, i, 0))

    def prev(t):
        return pl.BlockSpec((None, None, H, dil, BAND, Dh),
                            lambda b, i: (t, b, 0, 0, jnp.maximum(i * chunks - 1, 0), 0))

    scratch = [pltpu.VMEM((CONV_SLABS, tile_tokens + CONV_HALO, LANES), F32),
               pltpu.VMEM((CONV_SLABS, tile_tokens, LANES), F32)]
    if dil > 1:
        scratch.append(pltpu.VMEM((H, tile_tokens, Dh), F32))
    return pl.pallas_call(
        functools.partial(_attn_kernel, dil=dil),
        grid=(B, S // tile_tokens),
        in_specs=[cur(0), cur(1), prev(1), cur(2), prev(2),
                  pl.BlockSpec((None, tile_tokens, width), lambda b, i: (b, i, group)),
                  pl.BlockSpec((None, CONV_HALO, width),
                               lambda b, i: (b, jnp.maximum(i * halo_blocks - 1, 0), group)),
                  pl.BlockSpec((CONV_K, width), lambda b, i: (0, group)),
                  pl.BlockSpec((1, width), lambda b, i: (0, group)),
                  _const_spec((1, LANES))],
        out_specs=[pl.BlockSpec((None, H, tile_tokens, Dh), lambda b, i: (b, 0, i, 0)),
                   pl.BlockSpec((None, tile_tokens, LANES), lambda b, i: (b, i, 0)),
                   pl.BlockSpec((None, tile_tokens, width), lambda b, i: (b, i, 0))],
        out_shape=[jax.ShapeDtypeStruct((B, H, S, Dh), BF16),
                   jax.ShapeDtypeStruct((B, S, LANES), F32),
                   jax.ShapeDtypeStruct((B, S, width), BF16)],
        scratch_shapes=scratch,
        compiler_params=pltpu.CompilerParams(dimension_semantics=("parallel", "arbitrary"),
                                             vmem_limit_bytes=VMEM_LIMIT),
        name=f"dilated_attn_d{dil}",
    )(qkv, qkv, qkv, qkv, qkv, glu, glu, conv_w.astype(F32),
      conv_b.reshape(1, -1).astype(F32), jnp.zeros((1, LANES), jnp.int32))


def _mixer_kernel(o0_ref, o1_ref, o2_ref, s0_ref, s1_ref, s2_ref, c0_ref, c1_ref, c2_ref,
                  x_ref, g_ref, wgate_ref, bgate_ref, lg_ref, lb_ref, wap_ref, wcp_ref, wout_ref,
                  out_ref):
    d_model = x_ref.shape[1]
    xf = x_ref[...]
    u = (xf * _rms_scale(xf) * g_ref[...]).astype(BF16)
    gates = _sigmoid(_dot(u, wgate_ref[...]) + bgate_ref[...])

    lses = [s0_ref[...], s1_ref[...], s2_ref[...]]
    top = jnp.maximum(jnp.maximum(lses[0], lses[1]), lses[2])
    es = [jnp.exp(s - top) for s in lses]
    inv = 1.0 / (es[0] + es[1] + es[2])
    wts = [e * inv for e in es]
    heads = []
    for h in range(HEADS_PER_GROUP):
        acc = None
        for g, o_ref in enumerate((o0_ref, o1_ref, o2_ref)):
            term = wts[g][:, h:h + 1] * o_ref[h].astype(F32)
            acc = term if acc is None else acc + term
        heads.append(acc.astype(BF16))
    y_attn = _dot(jnp.concatenate(heads, axis=1), wap_ref[...])

    conv = jnp.concatenate([c0_ref[...], c1_ref[...], c2_ref[...]], axis=1).astype(F32)
    mu = jnp.mean(conv, axis=-1, keepdims=True)
    cen = conv - mu
    var = jnp.mean(cen * cen, axis=-1, keepdims=True)
    y = cen * lax.rsqrt(var + EPS) * lg_ref[...] + lb_ref[...]
    y_conv = _dot((y * _sigmoid(y)).astype(BF16), wcp_ref[...])

    merged = gates[:, :d_model] * y_attn + gates[:, d_model:] * y_conv
    out_ref[...] = xf + _dot(merged.astype(BF16), wout_ref[...])


def _mixer_merge(x, outs, stats, convs, g_mix, w_bf16, b_gate, ln_g, ln_b, w_attn_proj,
                 w_conv_proj, w_out, tm=512):
    B, S, D = x.shape
    conv_ch = sum(c.shape[-1] for c in convs)
    H = HEADS_PER_GROUP
    tiles = S // tm
    gate_block = w_bf16.shape[1] // (2 * D) - 1
    assert (gate_block + 1) * 2 * D == w_bf16.shape[1]
    o_spec = pl.BlockSpec((None, H, tm, HEAD_DIM), lambda b, i: (b, 0, i, 0))
    s_spec = pl.BlockSpec((None, tm, LANES), lambda b, i: (b, i, 0))
    c_specs = [pl.BlockSpec((None, tm, c.shape[-1]), lambda b, i: (b, i, 0)) for c in convs]
    row = lambda a: a.reshape(1, -1).astype(F32)
    return pl.pallas_call(
        _mixer_kernel,
        grid=(B, tiles),
        in_specs=[o_spec, o_spec, o_spec, s_spec, s_spec, s_spec] + c_specs + [
            pl.BlockSpec((None, tm, D), lambda b, i: (b, i, 0)),
            _const_spec((1, D)),
            pl.BlockSpec((D, 2 * D), lambda b, i: (0, gate_block), pipeline_mode=pl.Buffered(1)),
            _const_spec((1, 2 * D)),
            _const_spec((1, conv_ch)), _const_spec((1, conv_ch)),
            _const_spec((H * HEAD_DIM, D)),
            _const_spec((conv_ch, D)),
            _const_spec((D, D))],
        out_specs=pl.BlockSpec((None, tm, D), lambda b, i: (b, i, 0)),
        out_shape=jax.ShapeDtypeStruct((B, S, D), F32),
        compiler_params=pltpu.CompilerParams(dimension_semantics=("parallel", "arbitrary"),
                                             vmem_limit_bytes=VMEM_LIMIT),
        name="mixer_merge",
    )(outs[0], outs[1], outs[2], stats[0], stats[1], stats[2], convs[0], convs[1], convs[2], x,
      row(g_mix), w_bf16, row(b_gate), row(ln_g), row(ln_b),
      w_attn_proj.astype(BF16), w_conv_proj.astype(BF16), w_out.astype(BF16))


def _mem_kv_kernel(m_ref, g_ref, w_ref, o_ref):
    mf = m_ref[...]
    u = (mf * _rms_scale(mf) * g_ref[...]).astype(BF16)
    o_ref[...] = _dot(u, w_ref[...]).astype(BF16)


def _cross_kernel(x_ref, g_ref, wq_ref, k_ref, v_ref, wo_ref, out_ref):
    tm, d_model = x_ref.shape
    hd = d_model // CROSS_HEADS
    xf = x_ref[...]
    u = (xf * _rms_scale(xf) * g_ref[...]).astype(BF16)
    cq = (_dot(u, wq_ref[...]) * (hd ** -0.5)).astype(BF16)
    heads = []
    for h in range(CROSS_HEADS):
        cs = slice(h * hd, (h + 1) * hd)
        s = _dot_nt(cq[:, cs], k_ref[:, cs])
        m = jnp.max(s, axis=-1, keepdims=True)
        p = jnp.exp(s - m)
        l = jnp.sum(p, axis=-1, keepdims=True)
        heads.append((_dot(p.astype(BF16), v_ref[:, cs]) * (1.0 / l)).astype(BF16))
    out_ref[...] = xf + _dot(jnp.concatenate(heads, axis=1), wo_ref[...])


def _cross_attention(x, mem, g_cross, g_mem, w_cq, w_ckv, w_co, tm=1024):
    B, S, D = x.shape
    n_mem = mem.shape[1]
    params = pltpu.CompilerParams(dimension_semantics=("parallel", "arbitrary"),
                                  vmem_limit_bytes=VMEM_LIMIT)
    ckv = pl.pallas_call(
        _mem_kv_kernel,
        grid=(B, 2),
        in_specs=[pl.BlockSpec((None, n_mem, D), lambda b, j: (b, 0, 0)),
                  pl.BlockSpec((1, D), lambda b, j: (0, 0)),
                  pl.BlockSpec((D, D), lambda b, j: (0, j))],
        out_specs=pl.BlockSpec((None, n_mem, D), lambda b, j: (b, 0, j)),
        out_shape=jax.ShapeDtypeStruct((B, n_mem, 2 * D), BF16),
        compiler_params=params,
        name="mem_kv",
    )(mem, g_mem.reshape(1, D), w_ckv.astype(BF16))
    return pl.pallas_call(
        _cross_kernel,
        grid=(B, S // tm),
        in_specs=[pl.BlockSpec((None, tm, D), lambda b, i: (b, i, 0)),
                  _const_spec((1, D)),
                  _const_spec((D, D)),
                  pl.BlockSpec((None, n_mem, D), lambda b, i: (b, 0, 0)),
                  pl.BlockSpec((None, n_mem, D), lambda b, i: (b, 0, 1)),
                  _const_spec((D, D))],
        out_specs=pl.BlockSpec((None, tm, D), lambda b, i: (b, i, 0)),
        out_shape=jax.ShapeDtypeStruct((B, S, D), F32),
        compiler_params=params,
        name="cross_attn",
    )(x, g_cross.reshape(1, D), w_cq.astype(BF16), ckv, ckv, w_co.astype(BF16))


def _mlp_kernel(x_ref, g_ref, wu_ref, wd_ref, gf_ref, out_ref, *, final_norm, ff_chunk):
    xf = x_ref[...]
    u = (xf * _rms_scale(xf) * g_ref[...]).astype(BF16)
    acc = xf
    for c in range(wu_ref.shape[1] // ff_chunk):
        cs = slice(c * ff_chunk, (c + 1) * ff_chunk)
        h = jnp.maximum(_dot(u, wu_ref[:, cs]), 0.0)
        acc = acc + _dot((h * h).astype(BF16), wd_ref[cs, :])
    if final_norm:
        acc = acc * _rms_scale(acc) * gf_ref[...]
    out_ref[...] = acc


def _mlp(x, g_mlp, w_up, w_down, g_final, final_norm, tm=1024, ff_chunk=1024):
    B, S, D = x.shape
    d_ff = w_up.shape[1]
    return pl.pallas_call(
        functools.partial(_mlp_kernel, final_norm=final_norm, ff_chunk=ff_chunk),
        grid=(B, S // tm),
        in_specs=[pl.BlockSpec((None, tm, D), lambda b, i: (b, i, 0)),
                  _const_spec((1, D)),
                  _const_spec((D, d_ff)),
                  _const_spec((d_ff, D)),
                  _const_spec((1, D))],
        out_specs=pl.BlockSpec((None, tm, D), lambda b, i: (b, i, 0)),
        out_shape=jax.ShapeDtypeStruct((B, S, D), F32),
        compiler_params=pltpu.CompilerParams(dimension_semantics=("parallel", "arbitrary"),
                                             vmem_limit_bytes=VMEM_LIMIT),
        name="mlp",
    )(x, g_mlp.reshape(1, D), w_up.astype(BF16), w_down.astype(BF16), g_final.reshape(1, D))


def kernel(x, mem, g_mix, w_in, b_gate, conv_w, conv_b, conv_ln_g, conv_ln_b, w_attn_proj,
           w_conv_proj, w_out, g_cross, g_mem, w_cq, w_ckv, w_co, g_mlp, w_up, w_down, g_final):
    depth = w_in.shape[0]
    conv_ch = conv_w.shape[-1]
    assert conv_ch == N_GROUPS * CONV_SLABS * LANES
    for l in range(depth):
        w_bf16 = w_in[l].astype(BF16)
        qkvs, glu = _in_projection(x, g_mix[l], w_bf16, conv_ch)
        outs, stats, convs = [], [], []
        for group, (qkv, (_, dil)) in enumerate(zip(qkvs, DIL_GROUPS)):
            o_g, st_g, conv_g = _dilated_attention(qkv, dil, glu, conv_w[l], conv_b[l], group)
            outs.append(o_g)
            stats.append(st_g)
            convs.append(conv_g)
        x = _mixer_merge(x, outs, stats, convs, g_mix[l], w_bf16, b_gate[l], conv_ln_g[l],
                         conv_ln_b[l], w_attn_proj[l], w_conv_proj[l], w_out[l])
        x = _cross_attention(x, mem, g_cross[l], g_mem[l], w_cq[l], w_ckv[l], w_co[l])
        x = _mlp(x, g_mlp[l], w_up[l], w_down[l], g_final, final_norm=(l == depth - 1))
    return x
```

```python
import functools

import jax
import jax.numpy as jnp
import numpy as np
from jax import lax
from jax.experimental import pallas as pl
from jax.experimental.pallas import tpu as pltpu

F32 = jnp.float32
BF16 = jnp.bfloat16

HEAD_DIM = 128
HEADS_PER_GROUP = 4
DIL_GROUPS = ((128, 1), (512, 4), (2048, 16))
N_GROUPS = len(DIL_GROUPS)
GROUP_WIDTH = HEADS_PER_GROUP * HEAD_DIM
ROT_DIM = HEAD_DIM // 4
ROPE_THETA = 500000.0
CONV_K = 31
CROSS_HEADS = 4
EPS = 1e-6

LANES = 128
BAND = 128
CONV_HALO = 32
MASK_VALUE = -1e30
Q_SCALE = HEAD_DIM ** -0.5
VMEM_LIMIT = 56 * 1024 * 1024

for _win, _dil in DIL_GROUPS:
    assert _win // _dil == BAND


def _const_spec(shape):
    return pl.BlockSpec(shape, lambda *_: (0,) * len(shape), pipeline_mode=pl.Buffered(1))


def _rms_scale(xf):
    return lax.rsqrt(jnp.mean(xf * xf, axis=-1, keepdims=True) + EPS)


def _sigmoid(z):
    return 1.0 / (1.0 + jnp.exp(-z))


def _dot(a, b):
    return jnp.dot(a, b, preferred_element_type=F32)


def _dot_nt(a, b):
    return lax.dot_general(a, b, (((1,), (1,)), ((), ())), preferred_element_type=F32)


DEINTERLEAVE_SLABS = 8


def _in_proj_kernel(x_ref, g_ref, w_ref, tab_ref, qkv0_ref, qkv1_ref, qkv2_ref, glu_ref,
                    slab_ref):
    tm = x_ref.shape[0]
    conv_ch = glu_ref.shape[1]
    attn_w = N_GROUPS * GROUP_WIDTH
    xf = x_ref[...]
    u = (xf * _rms_scale(xf) * g_ref[...]).astype(BF16)

    z = _dot(u, w_ref[:, 3 * attn_w:3 * attn_w + 2 * conv_ch])
    glu_ref[...] = (z[:, :conv_ch] * _sigmoid(z[:, conv_ch:])).astype(BF16)

    half = ROT_DIM // 2
    tab = tab_ref[...]
    lane = lax.broadcasted_iota(jnp.int32, tab.shape, 1)
    cos = jnp.where(lane < ROT_DIM, tab, 1.0)
    sin_hi = jnp.where(jnp.logical_and(lane >= half, lane < ROT_DIM),
                       pltpu.roll(tab, HEAD_DIM - half, 1), 0.0)
    sin_lo = jnp.where(lane < half, pltpu.roll(tab, HEAD_DIM - 3 * half, 1), 0.0)
    slot = 0
    for t in range(3):
        for g, out_ref in enumerate((qkv0_ref, qkv1_ref, qkv2_ref)):
            dil = DIL_GROUPS[g][1]
            idx = t * N_GROUPS + g
            z = _dot(u, w_ref[:, idx * GROUP_WIDTH:(idx + 1) * GROUP_WIDTH])
            for h in range(HEADS_PER_GROUP):
                zh = z[:, h * HEAD_DIM:(h + 1) * HEAD_DIM]
                if t < 2:
                    up = pltpu.roll(zh, ROT_DIM // 2, 1)
                    down = pltpu.roll(zh, HEAD_DIM - ROT_DIM // 2, 1)
                    zh = zh * cos + up * sin_hi + down * sin_lo
                if t == 0:
                    zh = zh * Q_SCALE
                if dil == 1:
                    out_ref[t, h, 0] = zh.astype(BF16)
                else:
                    slab = slab_ref.at[slot % DEINTERLEAVE_SLABS]
                    slot += 1
                    slab[...] = zh
                    for r in range(dil):
                        out_ref[t, h, r] = slab[pl.ds(r, tm // dil, stride=dil), :].astype(BF16)


def _rope_table(seq):
    pos = np.arange(seq, dtype=np.float64)
    inv_freq = ROPE_THETA ** (-np.arange(0, ROT_DIM, 2, dtype=np.float64) / ROT_DIM)
    ang = pos[:, None] * inv_freq[None, :]
    cos, sin = np.cos(ang), np.sin(ang)
    zeros = np.zeros((seq, HEAD_DIM - 2 * ROT_DIM))
    return jnp.asarray(np.concatenate([cos, cos, sin, -sin, zeros], axis=1), dtype=F32)


def _in_projection(x, g_mix, w_bf16, conv_ch, tm=1024):
    B, S, D = x.shape
    T = B * S
    attn_w = N_GROUPS * GROUP_WIDTH
    assert w_bf16.shape[1] == 3 * attn_w + 2 * conv_ch + 2 * D
    tiles_per_seq = S // tm
    H = HEADS_PER_GROUP

    def qkv_spec(dil):
        return pl.BlockSpec(
            (3, None, H, dil, tm // dil, HEAD_DIM),
            lambda i: (0, i // tiles_per_seq, 0, 0, i % tiles_per_seq, 0))

    def qkv_shape(dil):
        return jax.ShapeDtypeStruct((3, B, H, dil, S // dil, HEAD_DIM), BF16)

    dils = [d for _, d in DIL_GROUPS]
    outs = pl.pallas_call(
        _in_proj_kernel,
        grid=(T // tm,),
        in_specs=[pl.BlockSpec((tm, D), lambda i: (i, 0)),
                  _const_spec((1, D)),
                  _const_spec(w_bf16.shape),
                  pl.BlockSpec((tm, HEAD_DIM), lambda i: (i % tiles_per_seq, 0))],
        out_specs=[qkv_spec(d) for d in dils] + [pl.BlockSpec((tm, conv_ch), lambda i: (i, 0))],
        out_shape=[qkv_shape(d) for d in dils] + [jax.ShapeDtypeStruct((T, conv_ch), BF16)],
        scratch_shapes=[pltpu.VMEM((DEINTERLEAVE_SLABS, tm, HEAD_DIM), F32)],
        compiler_params=pltpu.CompilerParams(
            dimension_semantics=("parallel",), vmem_limit_bytes=VMEM_LIMIT),
        name="in_proj",
    )(x.reshape(T, D), g_mix.reshape(1, D), w_bf16, _rope_table(S))
    return outs[:N_GROUPS], outs[N_GROUPS].reshape(B, S, conv_ch)


CONV_ROWS = 16
CONV_SLABS = 2


def _conv_task(ext_ref, conv_ref, cw_ref, cb_ref, zero_ref, cc, base, after):
    cs = slice(cc * LANES, (cc + 1) * LANES)
    half = CONV_ROWS // 2
    first = CONV_HALO - (CONV_K - 1)
    start = jnp.broadcast_to(cb_ref[:, cs], (half, LANES))
    if after is not None:
        nothing = pltpu.bitcast(after, jnp.int32) & zero_ref[...]
        start = start + pltpu.bitcast(nothing, F32)
    even, odd = start, start
    for j in range(CONV_K + 1):
        rows = ext_ref[cc, pl.ds(base + first + j, half, stride=2), :]
        if j < CONV_K:
            even = even + rows * cw_ref[j:j + 1, cs]
        if j > 0:
            odd = odd + rows * cw_ref[j - 1:j, cs]
    conv_ref[cc, pl.ds(base, half, stride=2), :] = even
    conv_ref[cc, pl.ds(base + 1, half, stride=2), :] = odd
    return odd


def _attn_kernel(q_ref, k_ref, kp_ref, v_ref, vp_ref, glu_ref, gluh_ref, cw_ref, cb_ref, zero_ref,
                 o_ref, st_ref, conv_out_ref, ext_ref, conv_ref, *scratch, dil):
    rows = q_ref.shape[2]
    n_chunks = rows // BAND
    tile_tokens = rows * dil
    tile = pl.program_id(1)
    r_id = lax.broadcasted_iota(jnp.int32, (BAND, 2 * BAND), 0)
    c_id = lax.broadcasted_iota(jnp.int32, (BAND, 2 * BAND), 1)
    band = jnp.logical_and(c_id >= r_id, c_id <= r_id + BAND)
    band_first = jnp.logical_and(band, jnp.logical_or(c_id >= BAND, tile > 0))
    lane = lax.broadcasted_iota(jnp.int32, (BAND, LANES), 1)
    st_ref[...] = jnp.zeros(st_ref.shape, F32)

    halo = jnp.where(tile > 0, gluh_ref[...].astype(F32), 0.0)
    cur = glu_ref[...].astype(F32)
    for cc in range(CONV_SLABS):
        cs = slice(cc * LANES, (cc + 1) * LANES)
        ext_ref[cc, 0:CONV_HALO, :] = halo[:, cs]
        ext_ref[cc, CONV_HALO:, :] = cur[:, cs]
    conv_rows_per_head = tile_tokens // HEADS_PER_GROUP

    def head_body(h, carry):
        after = [None] * CONV_SLABS
        tasks_per_block = conv_rows_per_head // CONV_ROWS // (dil * n_chunks)
        for r in range(dil):
            for n in range(n_chunks):
                for j in range(tasks_per_block):
                    row = ((r * n_chunks + n) * tasks_per_block + j) * CONV_ROWS
                    for cc in range(CONV_SLABS):
                        after[cc] = _conv_task(ext_ref, conv_ref, cw_ref, cb_ref, zero_ref, cc,
                                               h * conv_rows_per_head + row, after[cc])
                rs = slice(n * BAND, (n + 1) * BAND)
                q = q_ref[h, r, rs, :]
                if n == 0:
                    k = jnp.concatenate([kp_ref[h, r], k_ref[h, r, rs, :]], axis=0)
                    v = jnp.concatenate([vp_ref[h, r], v_ref[h, r, rs, :]], axis=0)
                    mask = band_first
                else:
                    ks = slice((n - 1) * BAND, (n + 1) * BAND)
                    k = k_ref[h, r, ks, :]
                    v = v_ref[h, r, ks, :]
                    mask = band
                s = jnp.where(mask, _dot_nt(q, k), MASK_VALUE)
                m = jnp.max(s, axis=-1, keepdims=True)
                p = jnp.exp(s - m)
                l = jnp.sum(p, axis=-1, keepdims=True)
                o = _dot(p.astype(BF16), v) * (1.0 / l)
                lse = m + jnp.log(l)
                if dil == 1:
                    o_ref[h, rs, :] = o.astype(BF16)
                    st_ref[rs, :] = jnp.where(lane == h, lse, st_ref[rs, :])
                else:
                    ts = pl.ds(n * BAND * dil + r, BAND, stride=dil)
                    scratch[0][h, ts, :] = o
                    st_ref[ts, :] = jnp.where(lane == h, lse, st_ref[ts, :])
        if dil > 1:
            o_ref[h] = scratch[0][h].astype(BF16)
        return carry

    lax.fori_loop(0, HEADS_PER_GROUP, head_body, 0)
    for cc in range(CONV_SLABS):
        conv_out_ref[:, cc * LANES:(cc + 1) * LANES] = conv_ref[cc].astype(BF16)


def _dilated_attention(qkv, dil, glu, conv_w, conv_b, group, tile_tokens=2048):
    _, B, H, _, Sd, Dh = qkv.shape
    S = Sd * dil
    rows = tile_tokens // dil
    chunks = rows // BAND
    width = CONV_SLABS * LANES
    halo_blocks = tile_tokens // CONV_HALO

    def cur(t):
        return pl.BlockSpec((None, None, H, dil, rows, Dh), lambda b, i: (t, b, 0, 0, i, 0))

    def prev(t):
        return pl.BlockSpec((None, None, H, dil, BAND, Dh),
                            lambda b, i: (t, b, 0, 0, jnp.maximum(i * chunks - 1, 0), 0))

    scratch = [pltpu.VMEM((CONV_SLABS, tile_tokens + CONV_HALO, LANES), F32),
               pltpu.VMEM((CONV_SLABS, tile_tokens, LANES), F32)]
    if dil > 1:
        scratch.append(pltpu.VMEM((H, tile_tokens, Dh), F32))
    return pl.pallas_call(
        functools.partial(_attn_kernel, dil=dil),
        grid=(B, S // tile_tokens),
        in_specs=[cur(0), cur(1), prev(1), cur(2), prev(2),
                  pl.BlockSpec((None, tile_tokens, width), lambda b, i: (b, i, group)),
                  pl.BlockSpec((None, CONV_HALO, width),
                               lambda b, i: (b, jnp.maximum(i * halo_blocks - 1, 0), group)),
                  pl.BlockSpec((CONV_K, width), lambda b, i: (0, group)),
                  pl.BlockSpec((1, width), lambda b, i: (0, group)),
                  _const_spec((1, LANES))],
        out_specs=[pl.BlockSpec((None, H, tile_tokens, Dh), lambda b, i: (b, 0, i, 0)),
                   pl.BlockSpec((None, tile_tokens, LANES), lambda b, i: (b, i, 0)),
                   pl.BlockSpec((None, tile_tokens, width), lambda b, i: (b, i, 0))],
        out_shape=[jax.ShapeDtypeStruct((B, H, S, Dh), BF16),
                   jax.ShapeDtypeStruct((B, S, LANES), F32),
                   jax.ShapeDtypeStruct((B, S, width), BF16)],
        scratch_shapes=scratch,
        compiler_params=pltpu.CompilerParams(dimension_semantics=("parallel", "arbitrary"),
                                             vmem_limit_bytes=VMEM_LIMIT),
        name=f"dilated_attn_d{dil}",
    )(qkv, qkv, qkv, qkv, qkv, glu, glu, conv_w.astype(F32),
      conv_b.reshape(1, -1).astype(F32), jnp.zeros((1, LANES), jnp.int32))


def _mixer_kernel(o0_ref, o1_ref, o2_ref, s0_ref, s1_ref, s2_ref, c0_ref, c1_ref, c2_ref,
                  x_ref, g_ref, wgate_ref, bgate_ref, lg_ref, lb_ref, wap_ref, wcp_ref, wout_ref,
                  out_ref):
    d_model = x_ref.shape[1]
    xf = x_ref[...]
    u = (xf * _rms_scale(xf) * g_ref[...]).astype(BF16)
    gates = _sigmoid(_dot(u, wgate_ref[...]) + bgate_ref[...])

    lses = [s0_ref[...], s1_ref[...], s2_ref[...]]
    top = jnp.maximum(jnp.maximum(lses[0], lses[1]), lses[2])
    es = [jnp.exp(s - top) for s in lses]
    inv = 1.0 / (es[0] + es[1] + es[2])
    wts = [e * inv for e in es]
    heads = []
    for h in range(HEADS_PER_GROUP):
        acc = None
        for g, o_ref in enumerate((o0_ref, o1_ref, o2_ref)):
            term = wts[g][:, h:h + 1] * o_ref[h].astype(F32)
            acc = term if acc is None else acc + term
        heads.append(acc.astype(BF16))
    y_attn = _dot(jnp.concatenate(heads, axis=1), wap_ref[...])

    conv = jnp.concatenate([c0_ref[...], c1_ref[...], c2_ref[...]], axis=1).astype(F32)
    mu = jnp.mean(conv, axis=-1, keepdims=True)
    cen = conv - mu
    var = jnp.mean(cen * cen, axis=-1, keepdims=True)
    y = cen * lax.rsqrt(var + EPS) * lg_ref[...] + lb_ref[...]
    y_conv = _dot((y * _sigmoid(y)).astype(BF16), wcp_ref[...])

    merged = gates[:, :d_model] * y_attn + gates[:, d_model:] * y_conv
    out_ref[...] = xf + _dot(merged.astype(BF16), wout_ref[...])


def _mixer_merge(x, outs, stats, convs, g_mix, w_bf16, b_gate, ln_g, ln_b, w_attn_proj,
                 w_conv_proj, w_out, tm=512):
    B, S, D = x.shape
    conv_ch = sum(c.shape[-1] for c in convs)
    H = HEADS_PER_GROUP
    tiles = S // tm
    gate_block = w_bf16.shape[1] // (2 * D) - 1
    assert (gate_block + 1) * 2 * D == w_bf16.shape[1]
    o_spec = pl.BlockSpec((None, H, tm, HEAD_DIM), lambda b, i: (b, 0, i, 0))
    s_spec = pl.BlockSpec((None, tm, LANES), lambda b, i: (b, i, 0))
    c_specs = [pl.BlockSpec((None, tm, c.shape[-1]), lambda b, i: (b, i, 0)) for c in convs]
    row = lambda a: a.reshape(1, -1).astype(F32)
    return pl.pallas_call(
        _mixer_kernel,
        grid=(B, tiles),
        in_specs=[o_spec, o_spec, o_spec, s_spec, s_spec, s_spec] + c_specs + [
            pl.BlockSpec((None, tm, D), lambda b, i: (b, i, 0)),
            _const_spec((1, D)),
            pl.BlockSpec((D, 2 * D), lambda b, i: (0, gate_block), pipeline_mode=pl.Buffered(1)),
            _const_spec((1, 2 * D)),
            _const_spec((1, conv_ch)), _const_spec((1, conv_ch)),
            _const_spec((H * HEAD_DIM, D)),
            _const_spec((conv_ch, D)),
            _const_spec((D, D))],
        out_specs=pl.BlockSpec((None, tm, D), lambda b, i: (b, i, 0)),
        out_shape=jax.ShapeDtypeStruct((B, S, D), F32),
        compiler_params=pltpu.CompilerParams(dimension_semantics=("parallel", "arbitrary"),
                                             vmem_limit_bytes=VMEM_LIMIT),
        name="mixer_merge",
    )(outs[0], outs[1], outs[2], stats[0], stats[1], stats[2], convs[0], convs[1], convs[2], x,
      row(g_mix), w_bf16, row(b_gate), row(ln_g), row(ln_b),
      w_attn_proj.astype(BF16), w_conv_proj.astype(BF16), w_out.astype(BF16))


def _mem_kv_kernel(m_ref, g_ref, w_ref, o_ref):
    mf = m_ref[...]
    u = (mf * _rms_scale(mf) * g_ref[...]).astype(BF16)
    o_ref[...] = _dot(u, w_ref[...]).astype(BF16)


def _cross_mlp_kernel(x_ref, gc_ref, wq_ref, k_ref, v_ref, wo_ref, gm_ref, wu_ref, wd_ref, gf_ref,
                      out_ref, *, final_norm, ff_chunk):
    d_model = x_ref.shape[1]
    hd = d_model // CROSS_HEADS
    xf = x_ref[...]
    u = (xf * _rms_scale(xf) * gc_ref[...]).astype(BF16)
    cq = (_dot(u, wq_ref[...]) * (hd ** -0.5)).astype(BF16)
    heads = []
    for h in range(CROSS_HEADS):
        cs = slice(h * hd, (h + 1) * hd)
        s = _dot_nt(cq[:, cs], k_ref[:, cs])
        m = jnp.max(s, axis=-1, keepdims=True)
        p = jnp.exp(s - m)
        l = jnp.sum(p, axis=-1, keepdims=True)
        heads.append((_dot(p.astype(BF16), v_ref[:, cs]) * (1.0 / l)).astype(BF16))
    x2 = xf + _dot(jnp.concatenate(heads, axis=1), wo_ref[...])

    u = (x2 * _rms_scale(x2) * gm_ref[...]).astype(BF16)
    acc = x2
    for c in range(wu_ref.shape[1] // ff_chunk):
        cs = slice(c * ff_chunk, (c + 1) * ff_chunk)
        h = jnp.maximum(_dot(u, wu_ref[:, cs]), 0.0)
        acc = acc + _dot((h * h).astype(BF16), wd_ref[cs, :])
    if final_norm:
        acc = acc * _rms_scale(acc) * gf_ref[...]
    out_ref[...] = acc


def _cross_attention_mlp(x, mem, g_cross, g_mem, w_cq, w_ckv, w_co, g_mlp, w_up, w_down, g_final,
                         final_norm, tm=1024, ff_chunk=1024):
    B, S, D = x.shape
    n_mem = mem.shape[1]
    d_ff = w_up.shape[1]
    params = pltpu.CompilerParams(dimension_semantics=("parallel", "arbitrary"),
                                  vmem_limit_bytes=VMEM_LIMIT)
    ckv = pl.pallas_call(
        _mem_kv_kernel,
        grid=(B, 2),
        in_specs=[pl.BlockSpec((None, n_mem, D), lambda b, j: (b, 0, 0)),
                  pl.BlockSpec((1, D), lambda b, j: (0, 0)),
                  pl.BlockSpec((D, D), lambda b, j: (0, j))],
        out_specs=pl.BlockSpec((None, n_mem, D), lambda b, j: (b, 0, j)),
        out_shape=jax.ShapeDtypeStruct((B, n_mem, 2 * D), BF16),
        compiler_params=params,
        name="mem_kv",
    )(mem, g_mem.reshape(1, D), w_ckv.astype(BF16))
    return pl.pallas_call(
        functools.partial(_cross_mlp_kernel, final_norm=final_norm, ff_chunk=ff_chunk),
        grid=(B, S // tm),
        in_specs=[pl.BlockSpec((None, tm, D), lambda b, i: (b, i, 0)),
                  _const_spec((1, D)),
                  _const_spec((D, D)),
                  pl.BlockSpec((None, n_mem, D), lambda b, i: (b, 0, 0)),
                  pl.BlockSpec((None, n_mem, D), lambda b, i: (b, 0, 1)),
                  _const_spec((D, D)),
                  _const_spec((1, D)),
                  _const_spec((D, d_ff)),
                  _const_spec((d_ff, D)),
                  _const_spec((1, D))],
        out_specs=pl.BlockSpec((None, tm, D), lambda b, i: (b, i, 0)),
        out_shape=jax.ShapeDtypeStruct((B, S, D), F32),
        compiler_params=params,
        name="cross_mlp",
    )(x, g_cross.reshape(1, D), w_cq.astype(BF16), ckv, ckv, w_co.astype(BF16),
      g_mlp.reshape(1, D), w_up.astype(BF16), w_down.astype(BF16), g_final.reshape(1, D))


def kernel(x, mem, g_mix, w_in, b_gate, conv_w, conv_b, conv_ln_g, conv_ln_b, w_attn_proj,
           w_conv_proj, w_out, g_cross, g_mem, w_cq, w_ckv, w_co, g_mlp, w_up, w_down, g_final):
    depth = w_in.shape[0]
    conv_ch = conv_w.shape[-1]
    assert conv_ch == N_GROUPS * CONV_SLABS * LANES
    for l in range(depth):
        w_bf16 = w_in[l].astype(BF16)
        qkvs, glu = _in_projection(x, g_mix[l], w_bf16, conv_ch)
        outs, stats, convs = [], [], []
        for group, (qkv, (_, dil)) in enumerate(zip(qkvs, DIL_GROUPS)):
            o_g, st_g, conv_g = _dilated_attention(qkv, dil, glu, conv_w[l], conv_b[l], group)
            outs.append(o_g)
            stats.append(st_g)
            convs.append(conv_g)
        x = _mixer_merge(x, outs, stats, convs, g_mix[l], w_bf16, b_gate[l], conv_ln_g[l],
                         conv_ln_b[l], w_attn_proj[l], w_conv_proj[l], w_out[l])
        x = _cross_attention_mlp(x, mem, g_cross[l], g_mem[l], w_cq[l], w_ckv[l], w_co[l],
                                 g_mlp[l], w_up[l], w_down[l], g_final,
                                 final_norm=(l == depth - 1))
    return x
```
